```python
import jax, jax.numpy as jnp
from jax import lax
import numpy as np

D_MODEL = 4096
BATCH = 8
SEQ = 2048
DEPTH = 4

N_MIXERS = 2
N_LAYERS_A = (DEPTH + N_MIXERS - 1) // N_MIXERS
N_LAYERS_B = DEPTH // N_MIXERS

CHUNK = 128
GMLP_GROUPS = 32
GMLP_WIDTH = D_MODEL
GMLP_GROUP_DIM = GMLP_WIDTH // GMLP_GROUPS

MLA_HEADS = 32
Q_LORA_RANK = 1024
KV_LORA_RANK = 512
QK_NOPE_DIM = 128
QK_ROPE_DIM = 64
V_HEAD_DIM = 128
ROPE_BASE = 10000.0
Q_BLOCK = 128

FFN_HIDDEN = -(-8 * D_MODEL // (3 * 256)) * 256

RMS_EPS = 1e-6
LN_EPS = 1e-5

kernel_name = 'hybrid_gmlp_mla_swiglu_sandwich'


def rms_norm(x, g):
    xf = x.astype(jnp.float32)
    y = xf * lax.rsqrt(jnp.mean(xf * xf, axis=-1, keepdims=True) + RMS_EPS)
    return (y * g.astype(jnp.float32)).astype(x.dtype)


def layer_norm(x, g, b):
    xf = x.astype(jnp.float32)
    mu = jnp.mean(xf, axis=-1, keepdims=True)
    xc = xf - mu
    y = xc * lax.rsqrt(jnp.mean(xc * xc, axis=-1, keepdims=True) + LN_EPS)
    return (y * g.astype(jnp.float32) + b.astype(jnp.float32)).astype(x.dtype)


def rope_tables(positions):
    inv_freq = ROPE_BASE ** (-jnp.arange(0, QK_ROPE_DIM, 2, dtype=jnp.float32) / QK_ROPE_DIM)
    ang = positions.astype(jnp.float32)[..., None] * inv_freq
    return jnp.cos(ang), jnp.sin(ang)


def apply_rope(x, cos, sin):
    xf = x.astype(jnp.float32)
    x1, x2 = jnp.split(xf, 2, axis=-1)
    out = jnp.concatenate([x1 * cos - x2 * sin, x2 * cos + x1 * sin], axis=-1)
    return out.astype(x.dtype)


def gmlp_mixer(h, w_in, ln_g, ln_b, w_s, b_s, w_out):
    B, S, _ = h.shape
    z = jax.nn.gelu(h @ w_in, approximate=False)
    u, v = jnp.split(z, 2, axis=-1)
    v = layer_norm(v, ln_g, ln_b)
    nc = S // CHUNK
    v = v.reshape(B, nc, CHUNK, GMLP_GROUPS, GMLP_GROUP_DIM)
    causal = jnp.tril(jnp.ones((CHUNK, CHUNK), dtype=bool))
    w = jnp.where(causal[None], w_s, jnp.zeros((), w_s.dtype))
    mixed = jnp.einsum('gts,bnsgc->bntgc', w, v) + b_s.T[None, None, :, :, None]
    y = u * mixed.reshape(B, S, GMLP_WIDTH)
    return y @ w_out


def mla_mixer(h, cos, sin, w_dqkv, q_norm_g, kv_norm_g, w_uq, w_ukv, w_o):
    B, S, _ = h.shape
    c = h @ w_dqkv
    c_q = c[..., :Q_LORA_RANK]
    c_kv = c[..., Q_LORA_RANK:Q_LORA_RANK + KV_LORA_RANK]
    k_rope = c[..., Q_LORA_RANK + KV_LORA_RANK:]
    c_q = rms_norm(c_q, q_norm_g)
    c_kv = rms_norm(c_kv, kv_norm_g)
    q = (c_q @ w_uq).reshape(B, S, MLA_HEADS, QK_NOPE_DIM + QK_ROPE_DIM)
    q_nope, q_rope = q[..., :QK_NOPE_DIM], q[..., QK_NOPE_DIM:]
    q_rope = apply_rope(q_rope, cos[:, :, None, :], sin[:, :, None, :])
    k_rope = apply_rope(k_rope, cos, sin)
    kv = (c_kv @ w_ukv).reshape(B, S, MLA_HEADS, QK_NOPE_DIM + V_HEAD_DIM)
    k_nope, v = kv[..., :QK_NOPE_DIM], kv[..., QK_NOPE_DIM:]
    scale = (QK_NOPE_DIM + QK_ROPE_DIM) ** -0.5
    nb = S // Q_BLOCK
    qn_blocks = q_nope.reshape(B, nb, Q_BLOCK, MLA_HEADS, QK_NOPE_DIM).transpose(1, 0, 2, 3, 4)
    qr_blocks = q_rope.reshape(B, nb, Q_BLOCK, MLA_HEADS, QK_ROPE_DIM).transpose(1, 0, 2, 3, 4)
    key_pos = jnp.arange(S)

    def attend(args):
        qn, qr, blk = args
        s = (jnp.einsum('bqhd,bkhd->bhqk', qn, k_nope)
             + jnp.einsum('bqhr,bkr->bhqk', qr, k_rope)).astype(jnp.float32) * scale
        q_pos = blk * Q_BLOCK + jnp.arange(Q_BLOCK)
        mask = key_pos[None, :] <= q_pos[:, None]
        s = jnp.where(mask[None, None], s, -jnp.inf)
        p = jax.nn.softmax(s, axis=-1).astype(v.dtype)
        return jnp.einsum('bhqk,bkhd->bqhd', p, v)

    o = lax.map(attend, (qn_blocks, qr_blocks, jnp.arange(nb)))
    o = o.transpose(1, 0, 2, 3, 4).reshape(B, S, MLA_HEADS * V_HEAD_DIM)
    return o @ w_o


def swiglu(h, w_gate_up, w_down):
    g, u = jnp.split(h @ w_gate_up, 2, axis=-1)
    return (jax.nn.silu(g) * u) @ w_down


def setup_inputs(seed: int = 0) -> dict:
    key = jax.random.key(seed)
    ks = jax.random.split(key, 20)
    f32 = jnp.float32

    def w(k, shape, fan_in):
        return jax.random.normal(k, shape, f32) * (fan_in ** -0.5)

    x = jax.random.normal(ks[0], (BATCH, SEQ, D_MODEL), f32)
    offset = jax.random.randint(ks[1], (BATCH, 1), 0, 4096, dtype=jnp.int32)
    positions = offset + jnp.arange(SEQ, dtype=jnp.int32)[None, :]
    norm_g = 1.0 + 0.02 * jax.random.normal(ks[2], (DEPTH, 4, D_MODEL), f32)

    gmlp_w_in = w(ks[3], (N_LAYERS_A, D_MODEL, 2 * GMLP_WIDTH), D_MODEL)
    gmlp_ln_g = 1.0 + 0.02 * jax.random.normal(ks[4], (N_LAYERS_A, GMLP_WIDTH), f32)
    gmlp_ln_b = 0.01 * jax.random.normal(ks[5], (N_LAYERS_A, GMLP_WIDTH), f32)
    gmlp_w_s = w(ks[6], (N_LAYERS_A, GMLP_GROUPS, CHUNK, CHUNK), CHUNK)
    gmlp_b_s = 1.0 + 0.01 * jax.random.normal(ks[7], (N_LAYERS_A, GMLP_GROUPS, CHUNK), f32)
    gmlp_w_out = w(ks[8], (N_LAYERS_A, GMLP_WIDTH, D_MODEL), GMLP_WIDTH)

    mla_w_dqkv = w(ks[9], (N_LAYERS_B, D_MODEL, Q_LORA_RANK + KV_LORA_RANK + QK_ROPE_DIM), D_MODEL)
    mla_q_norm_g = 1.0 + 0.02 * jax.random.normal(ks[10], (N_LAYERS_B, Q_LORA_RANK), f32)
    mla_kv_norm_g = 1.0 + 0.02 * jax.random.normal(ks[11], (N_LAYERS_B, KV_LORA_RANK), f32)
    mla_w_uq = w(ks[12], (N_LAYERS_B, Q_LORA_RANK, MLA_HEADS * (QK_NOPE_DIM + QK_ROPE_DIM)), Q_LORA_RANK)
    mla_w_ukv = w(ks[13], (N_LAYERS_B, KV_LORA_RANK, MLA_HEADS * (QK_NOPE_DIM + V_HEAD_DIM)), KV_LORA_RANK)
    mla_w_o = w(ks[14], (N_LAYERS_B, MLA_HEADS * V_HEAD_DIM, D_MODEL), MLA_HEADS * V_HEAD_DIM)

    ffn_w_gate_up = w(ks[15], (DEPTH, D_MODEL, 2 * FFN_HIDDEN), D_MODEL)
    ffn_w_down = w(ks[16], (DEPTH, FFN_HIDDEN, D_MODEL), FFN_HIDDEN)

    return {'x': x, 'positions': positions, 'norm_g': norm_g,
            'gmlp_w_in': gmlp_w_in, 'gmlp_ln_g': gmlp_ln_g, 'gmlp_ln_b': gmlp_ln_b,
            'gmlp_w_s': gmlp_w_s, 'gmlp_b_s': gmlp_b_s, 'gmlp_w_out': gmlp_w_out,
            'mla_w_dqkv': mla_w_dqkv, 'mla_q_norm_g': mla_q_norm_g, 'mla_kv_norm_g': mla_kv_norm_g,
            'mla_w_uq': mla_w_uq, 'mla_w_ukv': mla_w_ukv, 'mla_w_o': mla_w_o,
            'ffn_w_gate_up': ffn_w_gate_up, 'ffn_w_down': ffn_w_down}


def reference(x, positions, norm_g, gmlp_w_in, gmlp_ln_g, gmlp_ln_b, gmlp_w_s, gmlp_b_s,
              gmlp_w_out, mla_w_dqkv, mla_q_norm_g, mla_kv_norm_g, mla_w_uq, mla_w_ukv,
              mla_w_o, ffn_w_gate_up, ffn_w_down):
    cos, sin = rope_tables(positions)
    h = x
    for i in range(DEPTH):
        j = i // N_MIXERS
        a = rms_norm(h, norm_g[i, 0])
        if i % N_MIXERS == 0:
            m = gmlp_mixer(a, gmlp_w_in[j], gmlp_ln_g[j], gmlp_ln_b[j], gmlp_w_s[j],
                           gmlp_b_s[j], gmlp_w_out[j])
        else:
            m = mla_mixer(a, cos, sin, mla_w_dqkv[j], mla_q_norm_g[j], mla_kv_norm_g[j],
                          mla_w_uq[j], mla_w_ukv[j], mla_w_o[j])
        h = h + rms_norm(m, norm_g[i, 1])
        f = swiglu(rms_norm(h, norm_g[i, 2]), ffn_w_gate_up[i], ffn_w_down[i])
        h = h + rms_norm(f, norm_g[i, 3])
    return h
```

```python
import functools
import math

import numpy as np
import jax
import jax.numpy as jnp
from jax import lax
from jax.experimental import pallas as pl
from jax.experimental.pallas import tpu as pltpu

D_MODEL = 4096
DEPTH = 4
CHUNK = 128
GMLP_GROUPS = 32
GROUP_DIM = 128
HEADS = 32
Q_RANK = 1024
KV_RANK = 512
NOPE = 128
ROPE = 64
V_DIM = 128
ROPE_BASE = 10000.0
FFN_HIDDEN = 11008
RMS_EPS = 1e-6
LN_EPS = 1e-5

F32 = jnp.float32
BF16 = jnp.bfloat16

VMEM_LIMIT_BYTES = 56 * 1024 * 1024
LANES = 128


def _params(*semantics):
    return pltpu.CompilerParams(dimension_semantics=semantics,
                                vmem_limit_bytes=VMEM_LIMIT_BYTES)


def _rms(x, g):
    return x * lax.rsqrt(jnp.mean(x * x, axis=-1, keepdims=True) + RMS_EPS) * g


def _gelu_exact(x):
    return 0.5 * x * (1.0 + lax.erf(x * np.float32(math.sqrt(0.5))))


def _swap_halves_32(x):
    lane = lax.broadcasted_iota(jnp.int32, x.shape, 1)
    first_half = (lane % ROPE) < (ROPE // 2)
    return jnp.where(first_half, pltpu.roll(x, LANES - ROPE // 2, 1),
                     pltpu.roll(x, ROPE // 2, 1))


def _rope_table_kernel(pos_ref, freq_ref, sign_ref, cos_ref, sin_ref):
    ang = pos_ref[...].astype(F32) * freq_ref[...]
    cos_ref[...] = jnp.cos(ang)
    sin_ref[...] = jnp.sin(ang) * sign_ref[...]


def _rope_tables(positions):
    n = positions.size
    tm = min(2048, n)
    inv_freq = ROPE_BASE ** (-jnp.arange(0, ROPE, 2, dtype=F32) / ROPE)
    freq = jnp.tile(inv_freq, LANES // (ROPE // 2))[None, :]
    sign = jnp.tile(jnp.concatenate([-jnp.ones((ROPE // 2,), F32),
                                     jnp.ones((ROPE // 2,), F32)]), LANES // ROPE)[None, :]
    pos = jnp.broadcast_to(positions.reshape(n, 1), (n, LANES))
    row = pl.BlockSpec((tm, LANES), lambda i: (i, 0))
    const = pl.BlockSpec((1, LANES), lambda i: (0, 0))
    return pl.pallas_call(
        _rope_table_kernel,
        grid=(n // tm,),
        in_specs=[row, const, const],
        out_specs=[row, row],
        out_shape=[jax.ShapeDtypeStruct((n, LANES), F32)] * 2,
        compiler_params=_params("parallel"),
        name="rope_tables",
    )(pos, freq, sign)


def _prenorm_kernel(h_ref, g_ref, a_ref):
    a_ref[...] = _rms(h_ref[...], g_ref[...]).astype(BF16)


def _prenorm(h, g):
    m, d = h.shape
    tm = 256
    return pl.pallas_call(
        _prenorm_kernel,
        grid=(m // tm,),
        in_specs=[pl.BlockSpec((tm, d), lambda i: (i, 0)),
                  pl.BlockSpec((1, d), lambda i: (0, 0))],
        out_specs=pl.BlockSpec((tm, d), lambda i: (i, 0)),
        out_shape=jax.ShapeDtypeStruct((m, d), BF16),
        compiler_params=_params("parallel"),
        name="prenorm",
    )(h, g[None, :])


def _postnorm_kernel(f_ref, h_ref, gp_ref, gn_ref, ho_ref, a_ref):
    h_new = h_ref[...] + _rms(f_ref[...], gp_ref[...])
    ho_ref[...] = h_new
    a_ref[...] = _rms(h_new, gn_ref[...]).astype(BF16)


def _postnorm_last_kernel(f_ref, h_ref, gp_ref, ho_ref):
    ho_ref[...] = h_ref[...] + _rms(f_ref[...], gp_ref[...])


def _postnorm(f, h, g_post, g_next):
    m, d = h.shape
    tm = 256
    row = pl.BlockSpec((tm, d), lambda i: (i, 0))
    vec = pl.BlockSpec((1, d), lambda i: (0, 0))
    if g_next is None:
        return pl.pallas_call(
            _postnorm_last_kernel,
            grid=(m // tm,),
            in_specs=[row, row, vec],
            out_specs=row,
            out_shape=jax.ShapeDtypeStruct((m, d), F32),
            compiler_params=_params("parallel"),
            name="postnorm_last",
        )(f, h, g_post[None, :]), None
    return pl.pallas_call(
        _postnorm_kernel,
        grid=(m // tm,),
        in_specs=[row, row, vec, vec],
        out_specs=[row, row],
        out_shape=[jax.ShapeDtypeStruct((m, d), F32),
                   jax.ShapeDtypeStruct((m, d), BF16)],
        compiler_params=_params("parallel"),
        name="postnorm",
    )(f, h, g_post[None, :], g_next[None, :])


def _mm_kernel(x_ref, w_ref, o_ref, *, epilogue):
    r = jnp.dot(x_ref[...], w_ref[...], preferred_element_type=F32)
    if epilogue is not None:
        r = epilogue(r)
    o_ref[...] = r.astype(o_ref.dtype)


def _mm_ksplit_kernel(x_ref, w_ref, o_ref):
    r = jnp.dot(x_ref[...], w_ref[...], preferred_element_type=F32)
    k = pl.program_id(2)

    @pl.when(k == 0)
    def _():
        o_ref[...] = r

    @pl.when(k != 0)
    def _():
        o_ref[...] += r


def _matmul(x, w, *, tm, tn, out_dtype, epilogue=None, nk=1, name="matmul"):
    m, k = x.shape
    _, n = w.shape
    tm = min(tm, m)
    if nk == 1:
        return pl.pallas_call(
            functools.partial(_mm_kernel, epilogue=epilogue),
            grid=(m // tm, n // tn),
            in_specs=[pl.BlockSpec((tm, k), lambda i, j: (i, 0)),
                      pl.BlockSpec((k, tn), lambda i, j: (0, j))],
            out_specs=pl.BlockSpec((tm, tn), lambda i, j: (i, j)),
            out_shape=jax.ShapeDtypeStruct((m, n), out_dtype),
            compiler_params=_params("parallel", "parallel"),
            name=name,
        )(x, w)
    assert epilogue is None and out_dtype == F32
    tk = k // nk
    return pl.pallas_call(
        _mm_ksplit_kernel,
        grid=(m // tm, n // tn, nk),
        in_specs=[pl.BlockSpec((tm, tk), lambda i, j, kk: (i, kk)),
                  pl.BlockSpec((tk, tn), lambda i, j, kk: (kk, j))],
        out_specs=pl.BlockSpec((tm, tn), lambda i, j, kk: (i, j)),
        out_shape=jax.ShapeDtypeStruct((m, n), F32),
        compiler_params=_params("parallel", "parallel", "arbitrary"),
        name=name,
    )(x, w)


FFN_TN = 256


def _swiglu_kernel(x_ref, w_ref, o_ref):
    r = jnp.dot(x_ref[...], w_ref[...], preferred_element_type=F32)
    g = r[:, :FFN_TN]
    u = r[:, FFN_TN:]
    o_ref[...] = (g * jax.nn.sigmoid(g) * u).astype(o_ref.dtype)


def _swiglu_up(x, w_inter, *, tm):
    m, k = x.shape
    tm = min(tm, m)
    nj = FFN_HIDDEN // FFN_TN
    return pl.pallas_call(
        _swiglu_kernel,
        grid=(m // tm, nj),
        in_specs=[pl.BlockSpec((tm, k), lambda i, j: (i, 0)),
                  pl.BlockSpec((k, 2 * FFN_TN), lambda i, j: (0, j))],
        out_specs=pl.BlockSpec((tm, FFN_TN), lambda i, j: (i, j)),
        out_shape=jax.ShapeDtypeStruct((m, FFN_HIDDEN), BF16),
        compiler_params=_params("parallel", "parallel"),
        name="swiglu_up",
    )(x, w_inter)


def _interleave_gate_up(w_gate_up):
    k = w_gate_up.shape[0]
    nj = FFN_HIDDEN // FFN_TN
    w = w_gate_up.astype(BF16).reshape(k, 2, nj, FFN_TN)
    return w.transpose(0, 2, 1, 3).reshape(k, 2 * FFN_HIDDEN)


GMLP_ROWS = 256


def _gmlp_spatial_kernel(u_ref, v_ref, lg_ref, lb_ref, ws_ref, bs_ref, y_ref):
    v = v_ref[...]
    mu = jnp.mean(v, axis=-1, keepdims=True)
    vc = v - mu
    vn = vc * lax.rsqrt(jnp.mean(vc * vc, axis=-1, keepdims=True) + LN_EPS)
    vn = (vn * lg_ref[...] + lb_ref[...]).astype(BF16)
    t_idx = lax.broadcasted_iota(jnp.int32, (CHUNK, CHUNK), 0)
    s_idx = lax.broadcasted_iota(jnp.int32, (CHUNK, CHUNK), 1)
    causal = s_idx <= t_idx
    for g in range(GMLP_GROUPS):
        w = jnp.where(causal, ws_ref[g], 0.0).astype(BF16)
        bias = bs_ref[:, g:g + 1]
        cols = slice(g * GROUP_DIM, (g + 1) * GROUP_DIM)
        for c in range(GMLP_ROWS // CHUNK):
            rows = slice(c * CHUNK, (c + 1) * CHUNK)
            mixed = jnp.dot(w, vn[rows, cols], preferred_element_type=F32) + bias
            y_ref[rows, cols] = (u_ref[rows, cols].astype(F32) * mixed).astype(BF16)


def _gmlp_spatial(u, v, ln_g, ln_b, w_s, b_s):
    m, d = u.shape
    tm = GMLP_ROWS
    row = pl.BlockSpec((tm, d), lambda i: (i, 0))
    vec = pl.BlockSpec((1, d), lambda i: (0, 0))
    return pl.pallas_call(
        _gmlp_spatial_kernel,
        grid=(m // tm,),
        in_specs=[row, row, vec, vec,
                  pl.BlockSpec((GMLP_GROUPS, CHUNK, CHUNK), lambda i: (0, 0, 0)),
                  pl.BlockSpec((CHUNK, GMLP_GROUPS), lambda i: (0, 0))],
        out_specs=row,
        out_shape=jax.ShapeDtypeStruct((m, d), BF16),
        compiler_params=_params("parallel"),
        name="gmlp_spatial",
    )(u, v, ln_g[None, :], ln_b[None, :], w_s, b_s.T)


def _mla_down_kernel(x_ref, w_ref, gq_ref, gkv_ref, cos_ref, sin_ref,
                     cq_ref, ckv_ref, kr_ref):
    c = jnp.dot(x_ref[...], w_ref[...], preferred_element_type=F32)
    cq_ref[...] = _rms(c[:, :Q_RANK], gq_ref[...]).astype(BF16)
    ckv_ref[...] = _rms(c[:, Q_RANK:Q_RANK + KV_RANK], gkv_ref[...]).astype(BF16)
    kr = c[:, Q_RANK + KV_RANK:]
    half = ROPE // 2
    swapped = jnp.concatenate([kr[:, half:], kr[:, :half]], axis=-1)
    kr_ref[...] = (kr * cos_ref[:, :ROPE] + swapped * sin_ref[:, :ROPE]).astype(BF16)


def _mla_down(a, w_dqkv, gq, gkv, cos_t, sin_t):
    m, k = a.shape
    n = w_dqkv.shape[1]
    tm = 512
    tab = pl.BlockSpec((tm, LANES), lambda i: (i, 0))
    return pl.pallas_call(
        _mla_down_kernel,
        grid=(m // tm,),
        in_specs=[pl.BlockSpec((tm, k), lambda i: (i, 0)),
                  pl.BlockSpec((k, n), lambda i: (0, 0)),
                  pl.BlockSpec((1, Q_RANK), lambda i: (0, 0)),
                  pl.BlockSpec((1, KV_RANK), lambda i: (0, 0)),
                  tab, tab],
        out_specs=[pl.BlockSpec((tm, Q_RANK), lambda i: (i, 0)),
                   pl.BlockSpec((tm, KV_RANK), lambda i: (i, 0)),
                   pl.BlockSpec((tm, ROPE), lambda i: (i, 0))],
        out_shape=[jax.ShapeDtypeStruct((m, Q_RANK), BF16),
                   jax.ShapeDtypeStruct((m, KV_RANK), BF16),
                   jax.ShapeDtypeStruct((m, ROPE), BF16)],
        compiler_params=_params("parallel"),
        name="mla_down",
    )(a, w_dqkv, gq[None, :], gkv[None, :], cos_t, sin_t)


def _q_rope_kernel(x_ref, w_ref, cos_ref, sin_ref, o_ref):
    r = jnp.dot(x_ref[...], w_ref[...], preferred_element_type=F32)
    cos = cos_ref[...]
    sin = sin_ref[...]
    for t in range(r.shape[1] // LANES):
        x = r[:, t * LANES:(t + 1) * LANES]
        o_ref[:, t * LANES:(t + 1) * LANES] = (
            x * cos + _swap_halves_32(x) * sin).astype(BF16)


def _q_rope(cq, w_rope, cos_t, sin_t):
    m, k = cq.shape
    n = w_rope.shape[1]
    tm, tn = min(1024, m), 1024
    tab =pl.BlockSpec((tm, LANES), lambda i, j: (i, 0))
    return pl.pallas_call(
        _q_rope_kernel,
        grid=(m // tm, n // tn),
        in_specs=[pl.BlockSpec((tm, k), lambda i, j: (i, 0)),
                  pl.BlockSpec((k, tn), lambda i, j: (0, j)),
                  tab, tab],
        out_specs=pl.BlockSpec((tm, tn), lambda i, j: (i, j)),
        out_shape=jax.ShapeDtypeStruct((m, n), BF16),
        compiler_params=_params("parallel", "parallel"),
        name="q_rope",
    )(cq, w_rope, cos_t, sin_t)


ATT_TQ = 256
ATT_TK = 256
HEADS_PER_STEP = 2


def _attention_kernel(qn_ref, qr_ref, kn_ref, kr_ref, v_ref, o_ref):
    qi = pl.program_id(2)
    scale = np.float32((NOPE + ROPE) ** -0.5)
    row = lax.broadcasted_iota(jnp.int32, (ATT_TQ, ATT_TK), 0)
    col = lax.broadcasted_iota(jnp.int32, (ATT_TQ, ATT_TK), 1)
    diag_mask = col <= row
    for h in range(HEADS_PER_STEP):
        q = jnp.concatenate([qn_ref[:, h * NOPE:(h + 1) * NOPE],
                             qr_ref[:, h * ROPE:(h + 1) * ROPE]], axis=-1)

        def scores(j):
            rows = pl.ds(pl.multiple_of(j * ATT_TK, ATT_TK), ATT_TK)
            k = jnp.concatenate([kn_ref[rows, h * NOPE:(h + 1) * NOPE],
                                 kr_ref[rows, :]], axis=-1)
            s = lax.dot_general(q, k, (((1,), (1,)), ((), ())),
                                preferred_element_type=F32) * scale
            return s, v_ref[rows, h * V_DIM:(h + 1) * V_DIM]

        def update(carry, s, v):
            m_prev, l_prev, acc = carry
            m_new = jnp.maximum(m_prev, jnp.max(s, axis=-1, keepdims=True))
            alpha = jnp.exp(m_prev - m_new)
            p = jnp.exp(s - m_new)
            l_new = alpha * l_prev + jnp.sum(p, axis=-1, keepdims=True)
            acc = alpha * acc + jnp.dot(p.astype(BF16), v, preferred_element_type=F32)
            return m_new, l_new, acc

        def body(j, carry):
            s, v = scores(j)
            return update(carry, s, v)

        init = (jnp.full((ATT_TQ, 1), -jnp.inf, F32),
                jnp.zeros((ATT_TQ, 1), F32),
                jnp.zeros((ATT_TQ, V_DIM), F32))
        carry = lax.fori_loop(0, qi, body, init)
        s, v = scores(qi)
        s = jnp.where(diag_mask, s, -jnp.inf)
        _, l, acc = update(carry, s, v)
        o_ref[:, h * V_DIM:(h + 1) * V_DIM] = (acc / l).astype(BF16)


def _attention(q_nope, q_rope, kv, k_rope, batch, seq):
    m = batch * seq
    nq = seq // ATT_TQ
    hp = HEADS // HEADS_PER_STEP
    wn = HEADS_PER_STEP * NOPE
    wr = HEADS_PER_STEP * ROPE
    wv = HEADS_PER_STEP * V_DIM
    v_off = HEADS * NOPE // wv
    return pl.pallas_call(
        _attention_kernel,
        grid=(batch, hp, nq),
        in_specs=[
            pl.BlockSpec((ATT_TQ, wn), lambda b, p, i: (b * nq + i, p)),
            pl.BlockSpec((ATT_TQ, wr), lambda b, p, i: (b * nq + i, p)),
            pl.BlockSpec((seq, wn), lambda b, p, i: (b, p)),
            pl.BlockSpec((seq, ROPE), lambda b, p, i: (b, 0)),
            pl.BlockSpec((seq, wv), lambda b, p, i: (b, v_off + p)),
        ],
        out_specs=pl.BlockSpec((ATT_TQ, wv), lambda b, p, i: (b * nq + i, p)),
        out_shape=jax.ShapeDtypeStruct((m, HEADS * V_DIM), BF16),
        compiler_params=_params("parallel", "parallel", "arbitrary"),
        name="mla_attention",
    )(q_nope, q_rope, kv, k_rope, kv)


def _gmlp_layer(a, w_in, ln_g, ln_b, w_s, b_s, w_out):
    w_in = w_in.astype(BF16)
    u = _matmul(a, w_in[:, :D_MODEL], tm=1024, tn=512, out_dtype=BF16,
                epilogue=_gelu_exact, name="gmlp_in_u")
    v = _matmul(a, w_in[:, D_MODEL:], tm=1024, tn=512, out_dtype=F32,
                epilogue=_gelu_exact, name="gmlp_in_v")
    y = _gmlp_spatial(u, v, ln_g, ln_b, w_s, b_s)
    return _matmul(y, w_out.astype(BF16), tm=1024, tn=512, out_dtype=F32, name="gmlp_out")


def _mla_layer(a, cos_t, sin_t, w_dqkv, gq, gkv, w_uq, w_ukv, w_o, batch, seq):
    cq, ckv, k_rope = _mla_down(a, w_dqkv.astype(BF16), gq, gkv, cos_t, sin_t)
    w_uq = w_uq.astype(BF16).reshape(Q_RANK, HEADS, NOPE + ROPE)
    w_q_nope = w_uq[:, :, :NOPE].reshape(Q_RANK, HEADS * NOPE)
    w_q_rope = w_uq[:, :, NOPE:].reshape(Q_RANK, HEADS * ROPE)
    w_ukv = w_ukv.astype(BF16).reshape(KV_RANK, HEADS, NOPE + V_DIM)
    w_kv = jnp.concatenate([w_ukv[:, :, :NOPE].reshape(KV_RANK, HEADS * NOPE),
                            w_ukv[:, :, NOPE:].reshape(KV_RANK, HEADS * V_DIM)], axis=1)
    q_nope = _matmul(cq, w_q_nope, tm=1024, tn=1024, out_dtype=BF16, name="q_nope")
    q_rope = _q_rope(cq, w_q_rope, cos_t, sin_t)
    kv = _matmul(ckv, w_kv, tm=1024, tn=2048, out_dtype=BF16, name="kv_up")
    o = _attention(q_nope, q_rope, kv, k_rope, batch, seq)
    return _matmul(o, w_o.astype(BF16), tm=1024, tn=512, out_dtype=F32, name="attn_out")


def _ffn(a, w_gate_up, w_down):
    hid = _swiglu_up(a, _interleave_gate_up(w_gate_up), tm=1024)
    return _matmul(hid, w_down.astype(BF16), tm=1024, tn=512, out_dtype=F32, nk=2,
                   name="ffn_down")


def kernel(x, positions, norm_g, gmlp_w_in, gmlp_ln_g, gmlp_ln_b, gmlp_w_s, gmlp_b_s,
           gmlp_w_out, mla_w_dqkv, mla_q_norm_g, mla_kv_norm_g, mla_w_uq, mla_w_ukv,
           mla_w_o, ffn_w_gate_up, ffn_w_down):
    batch, seq, d = x.shape
    cos_t, sin_t = _rope_tables(positions)
    h = x.reshape(batch * seq, d)
    a = _prenorm(h, norm_g[0, 0])
    for i in range(DEPTH):
        j = i // 2
        if i % 2 == 0:
            mix = _gmlp_layer(a, gmlp_w_in[j], gmlp_ln_g[j], gmlp_ln_b[j], gmlp_w_s[j],
                              gmlp_b_s[j], gmlp_w_out[j])
        else:
            mix = _mla_layer(a, cos_t, sin_t, mla_w_dqkv[j], mla_q_norm_g[j],
                             mla_kv_norm_g[j], mla_w_uq[j], mla_w_ukv[j], mla_w_o[j],
                             batch, seq)
        h, a = _postnorm(mix, h, norm_g[i, 1], norm_g[i, 2])
        f = _ffn(a, ffn_w_gate_up[i], ffn_w_down[i])
        g_next = norm_g[i + 1, 0] if i + 1 < DEPTH else None
        h, a = _postnorm(f, h, norm_g[i, 3], g_next)
    return h.reshape(batch, seq, d)
```

```python
import functools
import math

import numpy as np
import jax
import jax.numpy as jnp
from jax import lax
from jax.experimental import pallas as pl
from jax.experimental.pallas import tpu as pltpu

D_MODEL = 4096
DEPTH = 4
CHUNK = 128
GMLP_GROUPS = 32
GROUP_DIM = 128
HEADS = 32
Q_RANK = 1024
KV_RANK = 512
NOPE = 128
ROPE = 64
V_DIM = 128
ROPE_BASE = 10000.0
FFN_HIDDEN = 11008
RMS_EPS = 1e-6
LN_EPS = 1e-5

F32 = jnp.float32
BF16 = jnp.bfloat16

VMEM_LIMIT_BYTES = 56 * 1024 * 1024
LANES = 128
MXU_WIDTH = 256


def _params(*semantics):
    return pltpu.CompilerParams(dimension_semantics=semantics,
                                vmem_limit_bytes=VMEM_LIMIT_BYTES)


def _stacked_w_spec(k, tn, layer, col_block_offset=0):
    return pl.BlockSpec((None, k, tn), lambda i, j: (layer, 0, col_block_offset + j))


def _rms(x, g):
    return x * lax.rsqrt(jnp.mean(x * x, axis=-1, keepdims=True) + RMS_EPS) * g


def _gelu_exact(x):
    return 0.5 * x * (1.0 + lax.erf(x * np.float32(math.sqrt(0.5))))


def _swap_halves_32(x):
    lane = lax.broadcasted_iota(jnp.int32, x.shape, 1)
    first_half = (lane % ROPE) < (ROPE // 2)
    return jnp.where(first_half, pltpu.roll(x, LANES - ROPE // 2, 1),
                     pltpu.roll(x, ROPE // 2, 1))


def _rope_table_kernel(pos_ref, freq_ref, sign_ref, cos_ref, sin_ref):
    ang = pos_ref[...].astype(F32) * freq_ref[...]
    cos_ref[...] = jnp.cos(ang)
    sin_ref[...] = jnp.sin(ang) * sign_ref[...]


def _rope_tables(positions):
    n = positions.size
    tm = min(2048, n)
    inv_freq = ROPE_BASE ** (-jnp.arange(0, ROPE, 2, dtype=F32) / ROPE)
    freq = jnp.tile(inv_freq, LANES // (ROPE // 2))[None, :]
    sign = jnp.tile(jnp.concatenate([-jnp.ones((ROPE // 2,), F32),
                                     jnp.ones((ROPE // 2,), F32)]), LANES // ROPE)[None, :]
    pos = jnp.broadcast_to(positions.reshape(n, 1), (n, LANES))
    row = pl.BlockSpec((tm, LANES), lambda i: (i, 0))
    const = pl.BlockSpec((1, LANES), lambda i: (0, 0))
    return pl.pallas_call(
        _rope_table_kernel,
        grid=(n // tm,),
        in_specs=[row, const, const],
        out_specs=[row, row],
        out_shape=[jax.ShapeDtypeStruct((n, LANES), F32)] * 2,
        compiler_params=_params("parallel"),
        name="rope_tables",
    )(pos, freq, sign)


def _prenorm_kernel(h_ref, g_ref, a_ref):
    a_ref[...] = _rms(h_ref[...], g_ref[...]).astype(BF16)


def _prenorm(h, g):
    m, d = h.shape
    tm = 256
    return pl.pallas_call(
        _prenorm_kernel,
        grid=(m // tm,),
        in_specs=[pl.BlockSpec((tm, d), lambda i: (i, 0)),
                  pl.BlockSpec((1, d), lambda i: (0, 0))],
        out_specs=pl.BlockSpec((tm, d), lambda i: (i, 0)),
        out_shape=jax.ShapeDtypeStruct((m, d), BF16),
        compiler_params=_params("parallel"),
        name="prenorm",
    )(h, g[None, :])


def _postnorm_kernel(f_ref, h_ref, gp_ref, gn_ref, ho_ref, a_ref):
    h_new = h_ref[...] + _rms(f_ref[...], gp_ref[...])
    ho_ref[...] = h_new
    a_ref[...] = _rms(h_new, gn_ref[...]).astype(BF16)


def _postnorm_last_kernel(f_ref, h_ref, gp_ref, ho_ref):
    ho_ref[...] = h_ref[...] + _rms(f_ref[...], gp_ref[...])


def _postnorm(f, h, g_post, g_next):
    m, d = h.shape
    tm = 256
    row = pl.BlockSpec((tm, d), lambda i: (i, 0))
    vec = pl.BlockSpec((1, d), lambda i: (0, 0))
    if g_next is None:
        return pl.pallas_call(
            _postnorm_last_kernel,
            grid=(m // tm,),
            in_specs=[row, row, vec],
            out_specs=row,
            out_shape=jax.ShapeDtypeStruct((m, d), F32),
            compiler_params=_params("parallel"),
            name="postnorm_last",
        )(f, h, g_post[None, :]), None
    return pl.pallas_call(
        _postnorm_kernel,
        grid=(m // tm,),
        in_specs=[row, row, vec, vec],
        out_specs=[row, row],
        out_shape=[jax.ShapeDtypeStruct((m, d), F32),
                   jax.ShapeDtypeStruct((m, d), BF16)],
        compiler_params=_params("parallel"),
        name="postnorm",
    )(f, h, g_post[None, :], g_next[None, :])


def _mm_kernel(x_ref, w_ref, o_ref):
    o_ref[...] = jnp.dot(x_ref[...], w_ref[...],
                         preferred_element_type=F32).astype(o_ref.dtype)


def _mm_ksplit_kernel(x_ref, w_ref, o_ref):
    r = jnp.dot(x_ref[...], w_ref[...], preferred_element_type=F32)
    k = pl.program_id(2)

    @pl.when(k == 0)
    def _():
        o_ref[...] = r

    @pl.when(k != 0)
    def _():
        o_ref[...] += r


def _matmul(x, w, layer, *, tm, tn, out_dtype, nk=1, name="matmul"):
    m, k = x.shape
    n = w.shape[2]
    tm = min(tm, m)
    if nk == 1:
        return pl.pallas_call(
            _mm_kernel,
            grid=(m // tm, n // tn),
            in_specs=[pl.BlockSpec((tm, k), lambda i, j: (i, 0)),
                      _stacked_w_spec(k, tn, layer)],
            out_specs=pl.BlockSpec((tm, tn), lambda i, j: (i, j)),
            out_shape=jax.ShapeDtypeStruct((m, n), out_dtype),
            compiler_params=_params("parallel", "parallel"),
            name=name,
        )(x, w)
    assert out_dtype == F32
    tk = k // nk
    return pl.pallas_call(
        _mm_ksplit_kernel,
        grid=(m // tm, n // tn, nk),
        in_specs=[pl.BlockSpec((tm, tk), lambda i, j, kk: (i, kk)),
                  pl.BlockSpec((None, tk, tn), lambda i, j, kk: (layer, kk, j))],
        out_specs=pl.BlockSpec((tm, tn), lambda i, j, kk: (i, j)),
        out_shape=jax.ShapeDtypeStruct((m, n), F32),
        compiler_params=_params("parallel", "parallel", "arbitrary"),
        name=name,
    )(x, w)


def _swiglu_kernel(x_ref, wg_ref, wu_ref, o_ref):
    x = x_ref[...]
    g = jnp.dot(x, wg_ref[...], preferred_element_type=F32)
    u = jnp.dot(x, wu_ref[...], preferred_element_type=F32)
    o_ref[...] = (g * jax.nn.sigmoid(g) * u).astype(o_ref.dtype)


def _swiglu_up(x, w_gate_up, layer, *, tm):
    m, k = x.shape
    tm = min(tm, m)
    tn = MXU_WIDTH
    nj = FFN_HIDDEN // tn
    return pl.pallas_call(
        _swiglu_kernel,
        grid=(m // tm, nj),
        in_specs=[pl.BlockSpec((tm, k), lambda i, j: (i, 0)),
                  _stacked_w_spec(k, tn, layer),
                  _stacked_w_spec(k, tn, layer, nj)],
        out_specs=pl.BlockSpec((tm, tn), lambda i, j: (i, j)),
        out_shape=jax.ShapeDtypeStruct((m, FFN_HIDDEN), BF16),
        compiler_params=_params("parallel", "parallel"),
        name="swiglu_up",
    )(x, w_gate_up, w_gate_up)


def _gmlp_in_kernel(x_ref, wu_ref, wv_ref, u_ref, v_ref):
    x = x_ref[...]
    u_ref[...] = _gelu_exact(jnp.dot(x, wu_ref[...], preferred_element_type=F32)).astype(BF16)
    v_ref[...] = _gelu_exact(jnp.dot(x, wv_ref[...], preferred_element_type=F32))


def _gmlp_in(x, w_in, layer, *, tm, tn):
    m, k = x.shape
    tm = min(tm, m)
    nj = D_MODEL // tn
    out = pl.BlockSpec((tm, tn), lambda i, j: (i, j))
    return pl.pallas_call(
        _gmlp_in_kernel,
        grid=(m // tm, nj),
        in_specs=[pl.BlockSpec((tm, k), lambda i, j: (i, 0)),
                  _stacked_w_spec(k, tn, layer),
                  _stacked_w_spec(k, tn, layer, nj)],
        out_specs=[out, out],
        out_shape=[jax.ShapeDtypeStruct((m, D_MODEL), BF16),
                   jax.ShapeDtypeStruct((m, D_MODEL), F32)],
        compiler_params=_params("parallel", "parallel"),
        name="gmlp_in",
    )(x, w_in, w_in)


GMLP_ROWS = 256


def _gmlp_spatial_kernel(u_ref, v_ref, lg_ref, lb_ref, ws_ref, bs_ref, y_ref):
    v = v_ref[...]
    mu = jnp.mean(v, axis=-1, keepdims=True)
    vc = v - mu
    vn = vc * lax.rsqrt(jnp.mean(vc * vc, axis=-1, keepdims=True) + LN_EPS)
    vn = (vn * lg_ref[...] + lb_ref[...]).astype(BF16)
    t_idx = lax.broadcasted_iota(jnp.int32, (CHUNK, CHUNK), 0)
    s_idx = lax.broadcasted_iota(jnp.int32, (CHUNK, CHUNK), 1)
    causal = s_idx <= t_idx
    for g in range(GMLP_GROUPS):
        w = jnp.where(causal, ws_ref[g], 0.0).astype(BF16)
        bias = bs_ref[:, g:g + 1]
        cols = slice(g * GROUP_DIM, (g + 1) * GROUP_DIM)
        for c in range(GMLP_ROWS // CHUNK):
            rows = slice(c * CHUNK, (c + 1) * CHUNK)
            mixed = jnp.dot(w, vn[rows, cols], preferred_element_type=F32) + bias
            y_ref[rows, cols] = (u_ref[rows, cols].astype(F32) * mixed).astype(BF16)


def _gmlp_spatial(u, v, ln_g, ln_b, w_s, b_s):
    m, d = u.shape
    tm = GMLP_ROWS
    row = pl.BlockSpec((tm, d), lambda i: (i, 0))
    vec = pl.BlockSpec((1, d), lambda i: (0, 0))
    return pl.pallas_call(
        _gmlp_spatial_kernel,
        grid=(m // tm,),
        in_specs=[row, row, vec, vec,
                  pl.BlockSpec((GMLP_GROUPS, CHUNK, CHUNK), lambda i: (0, 0, 0)),
                  pl.BlockSpec((CHUNK, GMLP_GROUPS), lambda i: (0, 0))],
        out_specs=row,
        out_shape=jax.ShapeDtypeStruct((m, d), BF16),
        compiler_params=_params("parallel"),
        name="gmlp_spatial",
    )(u, v, ln_g[None, :], ln_b[None, :], w_s, b_s.T)


def _mla_down_kernel(x_ref, w_ref, gq_ref, gkv_ref, cos_ref, sin_ref,
                     cq_ref, ckv_ref, kr_ref):
    c = jnp.dot(x_ref[...], w_ref[...], preferred_element_type=F32)
    cq_ref[...] = _rms(c[:, :Q_RANK], gq_ref[...]).astype(BF16)
    ckv_ref[...] = _rms(c[:, Q_RANK:Q_RANK + KV_RANK], gkv_ref[...]).astype(BF16)
    kr = c[:, Q_RANK + KV_RANK:]
    half = ROPE // 2
    swapped = jnp.concatenate([kr[:, half:], kr[:, :half]], axis=-1)
    kr_ref[...] = (kr * cos_ref[:, :ROPE] + swapped * sin_ref[:, :ROPE]).astype(BF16)


def _mla_down(a, w_dqkv, layer, gq, gkv, cos_t, sin_t):
    m, k = a.shape
    n = w_dqkv.shape[2]
    tm = 512
    tab = pl.BlockSpec((tm, LANES), lambda i: (i, 0))
    return pl.pallas_call(
        _mla_down_kernel,
        grid=(m // tm,),
        in_specs=[pl.BlockSpec((tm, k), lambda i: (i, 0)),
                  pl.BlockSpec((None, k, n), lambda i: (layer, 0, 0)),
                  pl.BlockSpec((1, Q_RANK), lambda i: (0, 0)),
                  pl.BlockSpec((1, KV_RANK), lambda i: (0, 0)),
                  tab, tab],
        out_specs=[pl.BlockSpec((tm, Q_RANK), lambda i: (i, 0)),
                   pl.BlockSpec((tm, KV_RANK), lambda i: (i, 0)),
                   pl.BlockSpec((tm, ROPE), lambda i: (i, 0))],
        out_shape=[jax.ShapeDtypeStruct((m, Q_RANK), BF16),
                   jax.ShapeDtypeStruct((m, KV_RANK), BF16),
                   jax.ShapeDtypeStruct((m, ROPE), BF16)],
        compiler_params=_params("parallel"),
        name="mla_down",
    )(a, w_dqkv, gq[None, :], gkv[None, :], cos_t, sin_t)


def _q_rope_kernel(x_ref, w_ref, cos_ref, sin_ref, o_ref):
    r = jnp.dot(x_ref[...], w_ref[...], preferred_element_type=F32)
    cos = cos_ref[...]
    sin = sin_ref[...]
    for t in range(r.shape[1] // LANES):
        x = r[:, t * LANES:(t + 1) * LANES]
        o_ref[:, t * LANES:(t + 1) * LANES] = (
            x * cos + _swap_halves_32(x) * sin).astype(BF16)


def _q_rope(cq, w_rope, layer, cos_t, sin_t):
    m, k = cq.shape
    n = w_rope.shape[2]
    tm, tn = min(1024, m), 1024
    tab = pl.BlockSpec((tm, LANES), lambda i, j: (i, 0))
    return pl.pallas_call(
        _q_rope_kernel,
        grid=(m // tm, n // tn),
        in_specs=[pl.BlockSpec((tm, k), lambda i, j: (i, 0)),
                  _stacked_w_spec(k, tn, layer),
                  tab, tab],
        out_specs=pl.BlockSpec((tm, tn), lambda i, j: (i, j)),
        out_shape=jax.ShapeDtypeStruct((m, n), BF16),
        compiler_params=_params("parallel", "parallel"),
        name="q_rope",
    )(cq, w_rope, cos_t, sin_t)


ATT_TQ = 512
ATT_TK = 512
HEADS_PER_STEP = 2
QK_DIM = NOPE + ROPE
QK_PAD = MXU_WIDTH
EXP2_SCALE = np.float32(QK_DIM ** -0.5 * math.log2(math.e))


def _attention_kernel(qn_ref, qr_ref, kn_ref, kr_ref, v_ref, o_ref,
                      q_scr, k_scr, m_scr, l_scr, acc_scr):
    qi = pl.program_id(2)
    seq = kn_ref.shape[0]

    @pl.when(qi == 0)
    def _():
        for h in range(HEADS_PER_STEP):
            k_scr[h, :, :NOPE] = kn_ref[:, h * NOPE:(h + 1) * NOPE]
            k_scr[h, :, NOPE:QK_DIM] = kr_ref[...]
            k_scr[h, :, QK_DIM:] = jnp.zeros((seq, QK_PAD - QK_DIM), BF16)

    for h in range(HEADS_PER_STEP):
        q_scr[h, :, :NOPE] = qn_ref[:, h * NOPE:(h + 1) * NOPE]
        q_scr[h, :, NOPE:QK_DIM] = qr_ref[:, h * ROPE:(h + 1) * ROPE]
        q_scr[h, :, QK_DIM:] = jnp.zeros((ATT_TQ, QK_PAD - QK_DIM), BF16)
    m_scr[...] = jnp.full(m_scr.shape, -jnp.inf, F32)
    l_scr[...] = jnp.zeros(l_scr.shape, F32)
    acc_scr[...] = jnp.zeros(acc_scr.shape, F32)

    def key_block(j, on_diagonal):
        rows = pl.ds(pl.multiple_of(j * ATT_TK, ATT_TK), ATT_TK)
        for h in range(HEADS_PER_STEP):
            s = lax.dot_general(q_scr[h], k_scr[h, rows, :], (((1,), (1,)), ((), ())),
                                preferred_element_type=F32)
            if on_diagonal:
                r_idx = lax.broadcasted_iota(jnp.int32, s.shape, 0)
                c_idx = lax.broadcasted_iota(jnp.int32, s.shape, 1)
                s = jnp.where(c_idx <= r_idx, s, -jnp.inf)
            m_prev = m_scr[h]
            m_new = jnp.maximum(m_prev, jnp.max(s, axis=1, keepdims=True))
            m_wide = jnp.concatenate([m_new] * (ATT_TK // LANES), axis=1)
            p = jnp.exp2((s - m_wide) * EXP2_SCALE)
            alpha = jnp.exp2((m_prev - m_new) * EXP2_SCALE)
            l_scr[h] = alpha * l_scr[h] + jnp.sum(p, axis=1, keepdims=True)
            pv = jnp.dot(p.astype(BF16), v_ref[rows, h * V_DIM:(h + 1) * V_DIM],
                         preferred_element_type=F32)
            acc_scr[h] = alpha * acc_scr[h] + pv
            m_scr[h] = m_new

    def body(j, carry):
        key_block(j, False)
        return carry

    lax.fori_loop(0, qi, body, 0)
    key_block(qi, True)
    for h in range(HEADS_PER_STEP):
        o_ref[:, h * V_DIM:(h + 1) * V_DIM] = (acc_scr[h] / l_scr[h]).astype(BF16)


def _attention(q_nope, q_rope, kv, k_rope, batch, seq):
    assert ATT_TQ == ATT_TK and V_DIM == LANES
    m = batch * seq
    nq = seq // ATT_TQ
    hp = HEADS // HEADS_PER_STEP
    wn = HEADS_PER_STEP * NOPE
    wr = HEADS_PER_STEP * ROPE
    wv = HEADS_PER_STEP * V_DIM
    v_off = HEADS * NOPE // wv
    return pl.pallas_call(
        _attention_kernel,
        grid=(batch, hp, nq),
        in_specs=[
            pl.BlockSpec((ATT_TQ, wn), lambda b, p, i: (b * nq + i, p)),
            pl.BlockSpec((ATT_TQ, wr), lambda b, p, i: (b * nq + i, p)),
            pl.BlockSpec((seq, wn), lambda b, p, i: (b, p)),
            pl.BlockSpec((seq, ROPE), lambda b, p, i: (b, 0)),
            pl.BlockSpec((seq, wv), lambda b, p, i: (b, v_off + p)),
        ],
        out_specs=pl.BlockSpec((ATT_TQ, wv), lambda b, p, i: (b * nq + i, p)),
        out_shape=jax.ShapeDtypeStruct((m, HEADS * V_DIM), BF16),
        scratch_shapes=[
            pltpu.VMEM((HEADS_PER_STEP, ATT_TQ, QK_PAD), BF16),
            pltpu.VMEM((HEADS_PER_STEP, seq, QK_PAD), BF16),
            pltpu.VMEM((HEADS_PER_STEP, ATT_TQ, LANES), F32),
            pltpu.VMEM((HEADS_PER_STEP, ATT_TQ, LANES), F32),
            pltpu.VMEM((HEADS_PER_STEP, ATT_TQ, V_DIM), F32),
        ],
        compiler_params=_params("parallel", "parallel", "arbitrary"),
        name="mla_attention",
    )(q_nope, q_rope, kv, k_rope, kv)


def _gmlp_layer(a, j, w_in, ln_g, ln_b, w_s, b_s, w_out):
    u, v = _gmlp_in(a, w_in, j, tm=1024, tn=512)
    y = _gmlp_spatial(u, v, ln_g[j], ln_b[j], w_s[j], b_s[j])
    return _matmul(y, w_out, j, tm=1024, tn=512, out_dtype=F32, name="gmlp_out")


def _mla_layer(a, j, cos_t, sin_t, w_dqkv, gq, gkv, w_q_nope, w_q_rope, w_kv, w_o,
               batch, seq):
    cq, ckv, k_rope = _mla_down(a, w_dqkv, j, gq[j], gkv[j], cos_t, sin_t)
    q_nope = _matmul(cq, w_q_nope, j, tm=1024, tn=1024, out_dtype=BF16, name="q_nope")
    q_rope = _q_rope(cq, w_q_rope, j, cos_t, sin_t)
    kv = _matmul(ckv, w_kv, j, tm=1024, tn=2048, out_dtype=BF16, name="kv_up")
    o = _attention(q_nope, q_rope, kv, k_rope, batch, seq)
    return _matmul(o, w_o, j, tm=1024, tn=512, out_dtype=F32, name="attn_out")


def _ffn(a, i, w_gate_up, w_down):
    hid = _swiglu_up(a, w_gate_up, i, tm=1024)
    return _matmul(hid, w_down, i, tm=1024, tn=512, out_dtype=F32, nk=2, name="ffn_down")


def kernel(x, positions, norm_g, gmlp_w_in, gmlp_ln_g, gmlp_ln_b, gmlp_w_s, gmlp_b_s,
           gmlp_w_out, mla_w_dqkv, mla_q_norm_g, mla_kv_norm_g, mla_w_uq, mla_w_ukv,
           mla_w_o, ffn_w_gate_up, ffn_w_down):
    batch, seq, d = x.shape
    n_mla = mla_w_uq.shape[0]
    gmlp_w_in = gmlp_w_in.astype(BF16)
    gmlp_w_out = gmlp_w_out.astype(BF16)
    mla_w_dqkv = mla_w_dqkv.astype(BF16)
    mla_w_o = mla_w_o.astype(BF16)
    ffn_w_gate_up = ffn_w_gate_up.astype(BF16)
    ffn_w_down = ffn_w_down.astype(BF16)
    w_uq = mla_w_uq.astype(BF16).reshape(n_mla, Q_RANK, HEADS, NOPE + ROPE)
    w_q_nope = w_uq[..., :NOPE].reshape(n_mla, Q_RANK, HEADS * NOPE)
    w_q_rope = w_uq[..., NOPE:].reshape(n_mla, Q_RANK, HEADS * ROPE)
    w_ukv = mla_w_ukv.astype(BF16).reshape(n_mla, KV_RANK, HEADS, NOPE + V_DIM)
    w_kv = jnp.concatenate([w_ukv[..., :NOPE].reshape(n_mla, KV_RANK, HEADS * NOPE),
                            w_ukv[..., NOPE:].reshape(n_mla, KV_RANK, HEADS * V_DIM)], axis=2)

    cos_t, sin_t = _rope_tables(positions)
    h = x.reshape(batch * seq, d)
    a = _prenorm(h, norm_g[0, 0])
    for i in range(DEPTH):
        j = i // 2
        if i % 2 == 0:
            mix = _gmlp_layer(a, j, gmlp_w_in, gmlp_ln_g, gmlp_ln_b, gmlp_w_s, gmlp_b_s,
                              gmlp_w_out)
        else:
            mix = _mla_layer(a, j, cos_t, sin_t, mla_w_dqkv, mla_q_norm_g, mla_kv_norm_g,
                             w_q_nope, w_q_rope, w_kv, mla_w_o, batch, seq)
        h, a = _postnorm(mix, h, norm_g[i, 1], norm_g[i, 2])
        f = _ffn(a, i, ffn_w_gate_up, ffn_w_down)
        g_next = norm_g[i + 1, 0] if i + 1 < DEPTH else None
        h, a = _postnorm(f, h, norm_g[i, 3], g_next)
    return h.reshape(batch, seq, d)
```

```python
import functools
import math

import numpy as np
import jax
import jax.numpy as jnp
from jax import lax
from jax.experimental import pallas as pl
from jax.experimental.pallas import tpu as pltpu

D_MODEL = 4096
DEPTH = 4
CHUNK = 128
GMLP_GROUPS = 32
GROUP_DIM = 128
HEADS = 32
Q_RANK = 1024
KV_RANK = 512
NOPE = 128
ROPE = 64
V_DIM = 128
ROPE_BASE = 10000.0
FFN_HIDDEN = 11008
RMS_EPS = 1e-6
LN_EPS = 1e-5

F32 = jnp.float32
BF16 = jnp.bfloat16

VMEM_LIMIT_BYTES = 56 * 1024 * 1024
LANES = 128
MXU_WIDTH = 256


def _params(*semantics):
    return pltpu.CompilerParams(dimension_semantics=semantics,
                                vmem_limit_bytes=VMEM_LIMIT_BYTES)


def _stacked_w_spec(k, tn, layer, col_block_offset=0):
    return pl.BlockSpec((None, k, tn), lambda i, j: (layer, 0, col_block_offset + j))


def _rms(x, g):
    return x * lax.rsqrt(jnp.mean(x * x, axis=-1, keepdims=True) + RMS_EPS) * g


def _gelu_exact(x):
    return 0.5 * x * (1.0 + lax.erf(x * np.float32(math.sqrt(0.5))))


def _swap_halves_32(x):
    lane = lax.broadcasted_iota(jnp.int32, x.shape, 1)
    first_half = (lane % ROPE) < (ROPE // 2)
    return jnp.where(first_half, pltpu.roll(x, LANES - ROPE // 2, 1),
                     pltpu.roll(x, ROPE // 2, 1))


def _rope_table_kernel(pos_ref, freq_ref, sign_ref, cos_ref, sin_ref):
    ang = pos_ref[...].astype(F32) * freq_ref[...]
    cos_ref[...] = jnp.cos(ang)
    sin_ref[...] = jnp.sin(ang) * sign_ref[...]


def _rope_tables(positions):
    n = positions.size
    tm = min(2048, n)
    inv_freq = ROPE_BASE ** (-jnp.arange(0, ROPE, 2, dtype=F32) / ROPE)
    freq = jnp.tile(inv_freq, LANES // (ROPE // 2))[None, :]
    sign = jnp.tile(jnp.concatenate([-jnp.ones((ROPE // 2,), F32),
                                     jnp.ones((ROPE // 2,), F32)]), LANES // ROPE)[None, :]
    pos = jnp.broadcast_to(positions.reshape(n, 1), (n, LANES))
    row = pl.BlockSpec((tm, LANES), lambda i: (i, 0))
    const = pl.BlockSpec((1, LANES), lambda i: (0, 0))
    return pl.pallas_call(
        _rope_table_kernel,
        grid=(n // tm,),
        in_specs=[row, const, const],
        out_specs=[row, row],
        out_shape=[jax.ShapeDtypeStruct((n, LANES), F32)] * 2,
        compiler_params=_params("parallel"),
        name="rope_tables",
    )(pos, freq, sign)


def _prenorm_kernel(h_ref, g_ref, a_ref):
    a_ref[...] = _rms(h_ref[...], g_ref[...]).astype(BF16)


def _prenorm(h, g):
    m, d = h.shape
    tm = 256
    return pl.pallas_call(
        _prenorm_kernel,
        grid=(m // tm,),
        in_specs=[pl.BlockSpec((tm, d), lambda i: (i, 0)),
                  pl.BlockSpec((1, d), lambda i: (0, 0))],
        out_specs=pl.BlockSpec((tm, d), lambda i: (i, 0)),
        out_shape=jax.ShapeDtypeStruct((m, d), BF16),
        compiler_params=_params("parallel"),
        name="prenorm",
    )(h, g[None, :])


def _postnorm_kernel(f_ref, h_ref, gp_ref, gn_ref, ho_ref, a_ref):
    h_new = h_ref[...] + _rms(f_ref[...], gp_ref[...])
    ho_ref[...] = h_new
    a_ref[...] = _rms(h_new, gn_ref[...]).astype(BF16)


def _postnorm_last_kernel(f_ref, h_ref, gp_ref, ho_ref):
    ho_ref[...] = h_ref[...] + _rms(f_ref[...], gp_ref[...])


def _postnorm(f, h, g_post, g_next):
    m, d = h.shape
    tm = 256
    row = pl.BlockSpec((tm, d), lambda i: (i, 0))
    vec = pl.BlockSpec((1, d), lambda i: (0, 0))
    if g_next is None:
        return pl.pallas_call(
            _postnorm_last_kernel,
            grid=(m // tm,),
            in_specs=[row, row, vec],
            out_specs=row,
            out_shape=jax.ShapeDtypeStruct((m, d), F32),
            compiler_params=_params("parallel"),
            name="postnorm_last",
        )(f, h, g_post[None, :]), None
    return pl.pallas_call(
        _postnorm_kernel,
        grid=(m // tm,),
        in_specs=[row, row, vec, vec],
        out_specs=[row, row],
        out_shape=[jax.ShapeDtypeStruct((m, d), F32),
                   jax.ShapeDtypeStruct((m, d), BF16)],
        compiler_params=_params("parallel"),
        name="postnorm",
    )(f, h, g_post[None, :], g_next[None, :])


def _mm_kernel(x_ref, w_ref, o_ref):
    o_ref[...] = jnp.dot(x_ref[...], w_ref[...],
                         preferred_element_type=F32).astype(o_ref.dtype)


def _matmul(x, w, layer, *, tm, tn, out_dtype, name="matmul"):
    m, k = x.shape
    n = w.shape[2]
    tm = min(tm, m)
    return pl.pallas_call(
        _mm_kernel,
        grid=(m // tm, n // tn),
        in_specs=[pl.BlockSpec((tm, k), lambda i, j: (i, 0)),
                  _stacked_w_spec(k, tn, layer)],
        out_specs=pl.BlockSpec((tm, tn), lambda i, j: (i, j)),
        out_shape=jax.ShapeDtypeStruct((m, n), out_dtype),
        compiler_params=_params("parallel", "parallel"),
        name=name,
    )(x, w)


def _swiglu_kernel(x_ref, wg_ref, wu_ref, o_ref):
    x = x_ref[...]
    g = jnp.dot(x, wg_ref[...].astype(BF16), preferred_element_type=F32)
    u = jnp.dot(x, wu_ref[...].astype(BF16), preferred_element_type=F32)
    o_ref[...] = (g * jax.nn.sigmoid(g) * u).astype(o_ref.dtype)


def _swiglu_up(x, w_gate_up, layer, *, tm):
    m, k = x.shape
    tm = min(tm, m)
    tn = MXU_WIDTH
    nj = FFN_HIDDEN // tn
    return pl.pallas_call(
        _swiglu_kernel,
        grid=(m // tm, nj),
        in_specs=[pl.BlockSpec((tm, k), lambda i, j: (i, 0),
                               pipeline_mode=pl.Buffered(1)),
                  _stacked_w_spec(k, tn, layer),
                  _stacked_w_spec(k, tn, layer, nj)],
        out_specs=pl.BlockSpec((tm, tn), lambda i, j: (i, j)),
        out_shape=jax.ShapeDtypeStruct((m, FFN_HIDDEN), BF16),
        compiler_params=_params("parallel", "parallel"),
        name="swiglu_up",
    )(x, w_gate_up, w_gate_up)


def _gmlp_in_kernel(x_ref, wu_ref, wv_ref, u_ref, v_ref):
    x = x_ref[...]
    u_ref[...] = _gelu_exact(jnp.dot(x, wu_ref[...], preferred_element_type=F32)).astype(BF16)
    v_ref[...] = _gelu_exact(jnp.dot(x, wv_ref[...], preferred_element_type=F32))


def _gmlp_in(x, w_in, layer, *, tm, tn):
    m, k = x.shape
    tm = min(tm, m)
    nj = D_MODEL // tn
    out = pl.BlockSpec((tm, tn), lambda i, j: (i, j))
    return pl.pallas_call(
        _gmlp_in_kernel,
        grid=(m // tm, nj),
        in_specs=[pl.BlockSpec((tm, k), lambda i, j: (i, 0)),
                  _stacked_w_spec(k, tn, layer),
                  _stacked_w_spec(k, tn, layer, nj)],
        out_specs=[out, out],
        out_shape=[jax.ShapeDtypeStruct((m, D_MODEL), BF16),
                   jax.ShapeDtypeStruct((m, D_MODEL), F32)],
        compiler_params=_params("parallel", "parallel"),
        name="gmlp_in",
    )(x, w_in, w_in)


GMLP_ROWS = 256


def _gmlp_spatial_kernel(u_ref, v_ref, lg_ref, lb_ref, ws_ref, bs_ref, y_ref):
    v = v_ref[...]
    mu = jnp.mean(v, axis=-1, keepdims=True)
    vc = v - mu
    vn = vc * lax.rsqrt(jnp.mean(vc * vc, axis=-1, keepdims=True) + LN_EPS)
    vn = (vn * lg_ref[...] + lb_ref[...]).astype(BF16)
    t_idx = lax.broadcasted_iota(jnp.int32, (CHUNK, CHUNK), 0)
    s_idx = lax.broadcasted_iota(jnp.int32, (CHUNK, CHUNK), 1)
    causal = s_idx <= t_idx
    for g in range(GMLP_GROUPS):
        w = jnp.where(causal, ws_ref[g], 0.0).astype(BF16)
        bias = bs_ref[:, g:g + 1]
        cols = slice(g * GROUP_DIM, (g + 1) * GROUP_DIM)
        for c in range(GMLP_ROWS // CHUNK):
            rows = slice(c * CHUNK, (c + 1) * CHUNK)
            mixed = jnp.dot(w, vn[rows, cols], preferred_element_type=F32) + bias
            y_ref[rows, cols] = (u_ref[rows, cols].astype(F32) * mixed).astype(BF16)


def _gmlp_spatial(u, v, ln_g, ln_b, w_s, b_s):
    m, d = u.shape
    tm = GMLP_ROWS
    row = pl.BlockSpec((tm, d), lambda i: (i, 0))
    vec = pl.BlockSpec((1, d), lambda i: (0, 0))
    return pl.pallas_call(
        _gmlp_spatial_kernel,
        grid=(m // tm,),
        in_specs=[row, row, vec, vec,
                  pl.BlockSpec((GMLP_GROUPS, CHUNK, CHUNK), lambda i: (0, 0, 0)),
                  pl.BlockSpec((CHUNK, GMLP_GROUPS), lambda i: (0, 0))],
        out_specs=row,
        out_shape=jax.ShapeDtypeStruct((m, d), BF16),
        compiler_params=_params("parallel"),
        name="gmlp_spatial",
    )(u, v, ln_g[None, :], ln_b[None, :], w_s, b_s.T)


def _mla_down_kernel(x_ref, w_ref, gq_ref, gkv_ref, cos_ref, sin_ref,
                     cq_ref, ckv_ref, kr_ref):
    c = jnp.dot(x_ref[...], w_ref[...], preferred_element_type=F32)
    cq_ref[...] = _rms(c[:, :Q_RANK], gq_ref[...]).astype(BF16)
    ckv_ref[...] = _rms(c[:, Q_RANK:Q_RANK + KV_RANK], gkv_ref[...]).astype(BF16)
    kr = c[:, Q_RANK + KV_RANK:]
    half = ROPE // 2
    swapped = jnp.concatenate([kr[:, half:], kr[:, :half]], axis=-1)
    kr_ref[...] = (kr * cos_ref[:, :ROPE] + swapped * sin_ref[:, :ROPE]).astype(BF16)


def _mla_down(a, w_dqkv, layer, gq, gkv, cos_t, sin_t):
    m, k = a.shape
    n = w_dqkv.shape[2]
    tm = 512
    tab = pl.BlockSpec((tm, LANES), lambda i: (i, 0))
    return pl.pallas_call(
        _mla_down_kernel,
        grid=(m // tm,),
        in_specs=[pl.BlockSpec((tm, k), lambda i: (i, 0)),
                  pl.BlockSpec((None, k, n), lambda i: (layer, 0, 0)),
                  pl.BlockSpec((1, Q_RANK), lambda i: (0, 0)),
                  pl.BlockSpec((1, KV_RANK), lambda i: (0, 0)),
                  tab, tab],
        out_specs=[pl.BlockSpec((tm, Q_RANK), lambda i: (i, 0)),
                   pl.BlockSpec((tm, KV_RANK), lambda i: (i, 0)),
                   pl.BlockSpec((tm, ROPE), lambda i: (i, 0))],
        out_shape=[jax.ShapeDtypeStruct((m, Q_RANK), BF16),
                   jax.ShapeDtypeStruct((m, KV_RANK), BF16),
                   jax.ShapeDtypeStruct((m, ROPE), BF16)],
        compiler_params=_params("parallel"),
        name="mla_down",
    )(a, w_dqkv, gq[None, :], gkv[None, :], cos_t, sin_t)


def _q_rope_kernel(x_ref, w_ref, cos_ref, sin_ref, o_ref):
    r = jnp.dot(x_ref[...], w_ref[...], preferred_element_type=F32)
    cos = cos_ref[...]
    sin = sin_ref[...]
    for t in range(r.shape[1] // LANES):
        x = r[:, t * LANES:(t + 1) * LANES]
        o_ref[:, t * LANES:(t + 1) * LANES] = (
            x * cos + _swap_halves_32(x) * sin).astype(BF16)


def _q_rope(cq, w_rope, layer, cos_t, sin_t):
    m, k = cq.shape
    n = w_rope.shape[2]
    tm, tn = min(1024, m), 1024
    tab = pl.BlockSpec((tm, LANES), lambda i, j: (i, 0))
    return pl.pallas_call(
        _q_rope_kernel,
        grid=(m // tm, n // tn),
        in_specs=[pl.BlockSpec((tm, k), lambda i, j: (i, 0)),
                  _stacked_w_spec(k, tn, layer),
                  tab, tab],
        out_specs=pl.BlockSpec((tm, tn), lambda i, j: (i, j)),
        out_shape=jax.ShapeDtypeStruct((m, n), BF16),
        compiler_params=_params("parallel", "parallel"),
        name="q_rope",
    )(cq, w_rope, cos_t, sin_t)


ATT_TQ = 512
ATT_TK = 512
HEADS_PER_STEP = 2
QK_DIM = NOPE + ROPE
QK_PAD = MXU_WIDTH
EXP2_SCALE = np.float32(QK_DIM ** -0.5 * math.log2(math.e))


def _attention_kernel(qn_ref, qr_ref, kn_ref, kr_ref, v_ref, o_ref,
                      q_scr, k_scr, m_scr, l_scr, acc_scr):
    qi = pl.program_id(2)
    seq = kn_ref.shape[0]

    @pl.when(qi == 0)
    def _():
        for h in range(HEADS_PER_STEP):
            k_scr[h, :, :NOPE] = kn_ref[:, h * NOPE:(h + 1) * NOPE]
            k_scr[h, :, NOPE:QK_DIM] = kr_ref[...]
            k_scr[h, :, QK_DIM:] = jnp.zeros((seq, QK_PAD - QK_DIM), BF16)

    for h in range(HEADS_PER_STEP):
        q_scr[h, :, :NOPE] = qn_ref[:, h * NOPE:(h + 1) * NOPE]
        q_scr[h, :, NOPE:QK_DIM] = qr_ref[:, h * ROPE:(h + 1) * ROPE]
        q_scr[h, :, QK_DIM:] = jnp.zeros((ATT_TQ, QK_PAD - QK_DIM), BF16)
    m_scr[...] = jnp.full(m_scr.shape, -jnp.inf, F32)
    l_scr[...] = jnp.zeros(l_scr.shape, F32)
    acc_scr[...] = jnp.zeros(acc_scr.shape, F32)

    def key_block(j, on_diagonal):
        rows = pl.ds(pl.multiple_of(j * ATT_TK, ATT_TK), ATT_TK)
        for h in range(HEADS_PER_STEP):
            s = lax.dot_general(q_scr[h], k_scr[h, rows, :], (((1,), (1,)), ((), ())),
                                preferred_element_type=F32)
            if on_diagonal:
                r_idx = lax.broadcasted_iota(jnp.int32, s.shape, 0)
                c_idx = lax.broadcasted_iota(jnp.int32, s.shape, 1)
                s = jnp.where(c_idx <= r_idx, s, -jnp.inf)
            m_prev = m_scr[h]
            m_new = jnp.maximum(m_prev, jnp.max(s, axis=1, keepdims=True))
            m_wide = jnp.concatenate([m_new] * (ATT_TK // LANES), axis=1)
            p = jnp.exp2((s - m_wide) * EXP2_SCALE)
            alpha = jnp.exp2((m_prev - m_new) * EXP2_SCALE)
            l_scr[h] = alpha * l_scr[h] + jnp.sum(p, axis=1, keepdims=True)
            pv = jnp.dot(p.astype(BF16), v_ref[rows, h * V_DIM:(h + 1) * V_DIM],
                         preferred_element_type=F32)
            acc_scr[h] = alpha * acc_scr[h] + pv
            m_scr[h] = m_new

    def body(j, carry):
        key_block(j, False)
        return carry

    lax.fori_loop(0, qi, body, 0)
    key_block(qi, True)
    for h in range(HEADS_PER_STEP):
        o_ref[:, h * V_DIM:(h + 1) * V_DIM] = (acc_scr[h] / l_scr[h]).astype(BF16)


def _attention(q_nope, q_rope, kv, k_rope, batch, seq):
    assert ATT_TQ == ATT_TK and V_DIM == LANES
    m = batch * seq
    nq = seq // ATT_TQ
    hp = HEADS // HEADS_PER_STEP
    wn = HEADS_PER_STEP * NOPE
    wr = HEADS_PER_STEP * ROPE
    wv = HEADS_PER_STEP * V_DIM
    v_off = HEADS * NOPE // wv
    return pl.pallas_call(
        _attention_kernel,
        grid=(batch, hp, nq),
        in_specs=[
            pl.BlockSpec((ATT_TQ, wn), lambda b, p, i: (b * nq + i, p)),
            pl.BlockSpec((ATT_TQ, wr), lambda b, p, i: (b * nq + i, p)),
            pl.BlockSpec((seq, wn), lambda b, p, i: (b, p)),
            pl.BlockSpec((seq, ROPE), lambda b, p, i: (b, 0)),
            pl.BlockSpec((seq, wv), lambda b, p, i: (b, v_off + p)),
        ],
        out_specs=pl.BlockSpec((ATT_TQ, wv), lambda b, p, i: (b * nq + i, p)),
        out_shape=jax.ShapeDtypeStruct((m, HEADS * V_DIM), BF16),
        scratch_shapes=[
            pltpu.VMEM((HEADS_PER_STEP, ATT_TQ, QK_PAD), BF16),
            pltpu.VMEM((HEADS_PER_STEP, seq, QK_PAD), BF16),
            pltpu.VMEM((HEADS_PER_STEP, ATT_TQ, LANES), F32),
            pltpu.VMEM((HEADS_PER_STEP, ATT_TQ, LANES), F32),
            pltpu.VMEM((HEADS_PER_STEP, ATT_TQ, V_DIM), F32),
        ],
        compiler_params=_params("parallel", "parallel", "arbitrary"),
        name="mla_attention",
    )(q_nope, q_rope, kv, k_rope, kv)


def _gmlp_layer(a, j, w_in, ln_g, ln_b, w_s, b_s, w_out):
    u, v = _gmlp_in(a, w_in, j, tm=1024, tn=512)
    y = _gmlp_spatial(u, v, ln_g[j], ln_b[j], w_s[j], b_s[j])
    return _matmul(y, w_out, j, tm=1024, tn=1024, out_dtype=F32, name="gmlp_out")


def _mla_layer(a, j, cos_t, sin_t, w_dqkv, gq, gkv, w_q_nope, w_q_rope, w_kv, w_o,
               batch, seq):
    cq, ckv, k_rope = _mla_down(a, w_dqkv, j, gq[j], gkv[j], cos_t, sin_t)
    q_nope = _matmul(cq, w_q_nope, j, tm=1024, tn=1024, out_dtype=BF16, name="q_nope")
    q_rope = _q_rope(cq, w_q_rope, j, cos_t, sin_t)
    kv = _matmul(ckv, w_kv, j, tm=1024, tn=2048, out_dtype=BF16, name="kv_up")
    o = _attention(q_nope, q_rope, kv, k_rope, batch, seq)
    return _matmul(o, w_o, j, tm=1024, tn=1024, out_dtype=F32, name="attn_out")


def _ffn(a, i, w_gate_up, w_down):
    hid = _swiglu_up(a, w_gate_up, i, tm=2048)
    return _matmul(hid, w_down, i, tm=512, tn=512, out_dtype=F32, name="ffn_down")


def kernel(x, positions, norm_g, gmlp_w_in, gmlp_ln_g, gmlp_ln_b, gmlp_w_s, gmlp_b_s,
           gmlp_w_out, mla_w_dqkv, mla_q_norm_g, mla_kv_norm_g, mla_w_uq, mla_w_ukv,
           mla_w_o, ffn_w_gate_up, ffn_w_down):
    batch, seq, d = x.shape
    n_mla = mla_w_uq.shape[0]
    gmlp_w_in = gmlp_w_in.astype(BF16)
    gmlp_w_out = gmlp_w_out.astype(BF16)
    mla_w_dqkv = mla_w_dqkv.astype(BF16)
    mla_w_o = mla_w_o.astype(BF16)
    ffn_w_down = ffn_w_down.astype(BF16)
    w_uq = mla_w_uq.astype(BF16).reshape(n_mla, Q_RANK, HEADS, NOPE + ROPE)
    w_q_nope = w_uq[..., :NOPE].reshape(n_mla, Q_RANK, HEADS * NOPE)
    w_q_rope = w_uq[..., NOPE:].reshape(n_mla, Q_RANK, HEADS * ROPE)
    w_ukv = mla_w_ukv.astype(BF16).reshape(n_mla, KV_RANK, HEADS, NOPE + V_DIM)
    w_kv = jnp.concatenate([w_ukv[..., :NOPE].reshape(n_mla, KV_RANK, HEADS * NOPE),
                            w_ukv[..., NOPE:].reshape(n_mla, KV_RANK, HEADS * V_DIM)], axis=2)

    cos_t, sin_t = _rope_tables(positions)
    h = x.reshape(batch * seq, d)
    a = _prenorm(h, norm_g[0, 0])
    for i in range(DEPTH):
        j = i // 2
        if i % 2 == 0:
            mix = _gmlp_layer(a, j, gmlp_w_in, gmlp_ln_g, gmlp_ln_b, gmlp_w_s, gmlp_b_s,
                              gmlp_w_out)
        else:
            mix = _mla_layer(a, j, cos_t, sin_t, mla_w_dqkv, mla_q_norm_g, mla_kv_norm_g,
                             w_q_nope, w_q_rope, w_kv, mla_w_o, batch, seq)
        h, a = _postnorm(mix, h, norm_g[i, 1], norm_g[i, 2])
        f = _ffn(a, i, ffn_w_gate_up, ffn_w_down)
        g_next = norm_g[i + 1, 0] if i + 1 < DEPTH else None
        h, a = _postnorm(f, h, norm_g[i, 3], g_next)
    return h.reshape(batch, seq, d)
```

```python
import functools
import math

import numpy as np
import jax
import jax.numpy as jnp
from jax import lax
from jax.experimental import pallas as pl
from jax.experimental.pallas import tpu as pltpu

D_MODEL = 4096
DEPTH = 4
CHUNK = 128
GMLP_GROUPS = 32
GROUP_DIM = 128
HEADS = 32
Q_RANK = 1024
KV_RANK = 512
NOPE = 128
ROPE = 64
V_DIM = 128
ROPE_BASE = 10000.0
FFN_HIDDEN = 11008
RMS_EPS = 1e-6
LN_EPS = 1e-5

F32 = jnp.float32
BF16 = jnp.bfloat16

VMEM_LIMIT_BYTES = 56 * 1024 * 1024
LANES = 128
MXU_WIDTH = 256


def _params(*semantics):
    return pltpu.CompilerParams(dimension_semantics=semantics,
                                vmem_limit_bytes=VMEM_LIMIT_BYTES)


def _stacked_w_spec(k, tn, layer, col_block_offset=0):
    return pl.BlockSpec((None, k, tn), lambda i, j: (layer, 0, col_block_offset + j))


def _rms(x, g):
    return x * lax.rsqrt(jnp.mean(x * x, axis=-1, keepdims=True) + RMS_EPS) * g


def _gelu_exact(x):
    return 0.5 * x * (1.0 + lax.erf(x * np.float32(math.sqrt(0.5))))


def _swap_halves_32(x):
    lane = lax.broadcasted_iota(jnp.int32, x.shape, 1)
    first_half = (lane % ROPE) < (ROPE // 2)
    return jnp.where(first_half, pltpu.roll(x, LANES - ROPE // 2, 1),
                     pltpu.roll(x, ROPE // 2, 1))


def _rope_table_kernel(pos_ref, freq_ref, sign_ref, cos_ref, sin_ref):
    ang = pos_ref[...].astype(F32) * freq_ref[...]
    cos_ref[...] = jnp.cos(ang)
    sin_ref[...] = jnp.sin(ang) * sign_ref[...]


def _rope_tables(positions):
    n = positions.size
    tm = min(2048, n)
    inv_freq = ROPE_BASE ** (-jnp.arange(0, ROPE, 2, dtype=F32) / ROPE)
    freq = jnp.tile(inv_freq, LANES // (ROPE // 2))[None, :]
    sign = jnp.tile(jnp.concatenate([-jnp.ones((ROPE // 2,), F32),
                                     jnp.ones((ROPE // 2,), F32)]), LANES // ROPE)[None, :]
    pos = jnp.broadcast_to(positions.reshape(n, 1), (n, LANES))
    row = pl.BlockSpec((tm, LANES), lambda i: (i, 0))
    const = pl.BlockSpec((1, LANES), lambda i: (0, 0))
    return pl.pallas_call(
        _rope_table_kernel,
        grid=(n // tm,),
        in_specs=[row, const, const],
        out_specs=[row, row],
        out_shape=[jax.ShapeDtypeStruct((n, LANES), F32)] * 2,
        compiler_params=_params("parallel"),
        name="rope_tables",
    )(pos, freq, sign)


def _prenorm_kernel(h_ref, g_ref, a_ref):
    a_ref[...] = _rms(h_ref[...], g_ref[...]).astype(BF16)


def _prenorm(h, g):
    m, d = h.shape
    tm = 256
    return pl.pallas_call(
        _prenorm_kernel,
        grid=(m // tm,),
        in_specs=[pl.BlockSpec((tm, d), lambda i: (i, 0)),
                  pl.BlockSpec((1, d), lambda i: (0, 0))],
        out_specs=pl.BlockSpec((tm, d), lambda i: (i, 0)),
        out_shape=jax.ShapeDtypeStruct((m, d), BF16),
        compiler_params=_params("parallel"),
        name="prenorm",
    )(h, g[None, :])


def _postnorm_kernel(f_ref, h_ref, gp_ref, gn_ref, ho_ref, a_ref):
    h_new = h_ref[...] + _rms(f_ref[...], gp_ref[...])
    ho_ref[...] = h_new
    a_ref[...] = _rms(h_new, gn_ref[...]).astype(BF16)


def _postnorm_last_kernel(f_ref, h_ref, gp_ref, ho_ref):
    ho_ref[...] = h_ref[...] + _rms(f_ref[...], gp_ref[...])


def _postnorm(f, h, g_post, g_next):
    m, d = h.shape
    tm = 256
    row = pl.BlockSpec((tm, d), lambda i: (i, 0))
    vec = pl.BlockSpec((1, d), lambda i: (0, 0))
    if g_next is None:
        return pl.pallas_call(
            _postnorm_last_kernel,
            grid=(m // tm,),
            in_specs=[row, row, vec],
            out_specs=row,
            out_shape=jax.ShapeDtypeStruct((m, d), F32),
            compiler_params=_params("parallel"),
            name="postnorm_last",
        )(f, h, g_post[None, :]), None
    return pl.pallas_call(
        _postnorm_kernel,
        grid=(m // tm,),
        in_specs=[row, row, vec, vec],
        out_specs=[row, row],
        out_shape=[jax.ShapeDtypeStruct((m, d), F32),
                   jax.ShapeDtypeStruct((m, d), BF16)],
        compiler_params=_params("parallel"),
        name="postnorm",
    )(f, h, g_post[None, :], g_next[None, :])


def _mm_kernel(x_ref, w_ref, o_ref):
    o_ref[...] = jnp.dot(x_ref[...], w_ref[...],
                         preferred_element_type=F32).astype(o_ref.dtype)


def _matmul(x, w, layer, *, tm, tn, out_dtype, name="matmul"):
    m, k = x.shape
    n = w.shape[2]
    tm = min(tm, m)
    return pl.pallas_call(
        _mm_kernel,
        grid=(m // tm, n // tn),
        in_specs=[pl.BlockSpec((tm, k), lambda i, j: (i, 0)),
                  _stacked_w_spec(k, tn, layer)],
        out_specs=pl.BlockSpec((tm, tn), lambda i, j: (i, j)),
        out_shape=jax.ShapeDtypeStruct((m, n), out_dtype),
        compiler_params=_params("parallel", "parallel"),
        name=name,
    )(x, w)


def _swiglu_kernel(x_ref, wg_ref, wu_ref, wd_ref, o_ref, wd_bf16_ref):
    x = x_ref[...]
    g = jnp.dot(x, wg_ref[...].astype(BF16), preferred_element_type=F32)
    u = jnp.dot(x, wu_ref[...].astype(BF16), preferred_element_type=F32)
    o_ref[...] = (g * jax.nn.sigmoid(g) * u).astype(o_ref.dtype)
    wd_bf16_ref[...] = wd_ref[...].astype(BF16)


def _swiglu_up(x, w_gate_up, w_down, layer, *, tm):
    m, k = x.shape
    tm = min(tm, m)
    tn = MXU_WIDTH
    nj = FFN_HIDDEN // tn
    n_steps = (m // tm) * nj
    assert FFN_HIDDEN % n_steps == 0
    slab = FFN_HIDDEN // n_steps
    d = w_down.shape[2]
    return pl.pallas_call(
        _swiglu_kernel,
        grid=(m // tm, nj),
        in_specs=[pl.BlockSpec((tm, k), lambda i, j: (i, 0),
                               pipeline_mode=pl.Buffered(1)),
                  _stacked_w_spec(k, tn, layer),
                  _stacked_w_spec(k, tn, layer, nj),
                  pl.BlockSpec((None, slab, d), lambda i, j: (layer, i * nj + j, 0))],
        out_specs=[pl.BlockSpec((tm, tn), lambda i, j: (i, j)),
                   pl.BlockSpec((None, slab, d), lambda i, j: (0, i * nj + j, 0))],
        out_shape=[jax.ShapeDtypeStruct((m, FFN_HIDDEN), BF16),
                   jax.ShapeDtypeStruct((1, FFN_HIDDEN, d), BF16)],
        compiler_params=_params("parallel", "parallel"),
        name="swiglu_up",
    )(x, w_gate_up, w_gate_up, w_down)


def _gmlp_in_kernel(x_ref, wu_ref, wv_ref, u_ref, v_ref):
    x = x_ref[...]
    u_ref[...] = _gelu_exact(jnp.dot(x, wu_ref[...], preferred_element_type=F32)).astype(BF16)
    v_ref[...] = _gelu_exact(jnp.dot(x, wv_ref[...], preferred_element_type=F32))


def _gmlp_in(x, w_in, layer, *, tm, tn):
    m, k = x.shape
    tm = min(tm, m)
    nj = D_MODEL // tn
    out = pl.BlockSpec((tm, tn), lambda i, j: (i, j))
    return pl.pallas_call(
        _gmlp_in_kernel,
        grid=(m // tm, nj),
        in_specs=[pl.BlockSpec((tm, k), lambda i, j: (i, 0)),
                  _stacked_w_spec(k, tn, layer),
                  _stacked_w_spec(k, tn, layer, nj)],
        out_specs=[out, out],
        out_shape=[jax.ShapeDtypeStruct((m, D_MODEL), BF16),
                   jax.ShapeDtypeStruct((m, D_MODEL), F32)],
        compiler_params=_params("parallel", "parallel"),
        name="gmlp_in",
    )(x, w_in, w_in)


GMLP_ROWS = 256


def _gmlp_spatial_kernel(u_ref, v_ref, lg_ref, lb_ref, ws_ref, bs_ref, y_ref):
    v = v_ref[...]
    mu = jnp.mean(v, axis=-1, keepdims=True)
    vc = v - mu
    vn = vc * lax.rsqrt(jnp.mean(vc * vc, axis=-1, keepdims=True) + LN_EPS)
    vn = (vn * lg_ref[...] + lb_ref[...]).astype(BF16)
    t_idx = lax.broadcasted_iota(jnp.int32, (CHUNK, CHUNK), 0)
    s_idx = lax.broadcasted_iota(jnp.int32, (CHUNK, CHUNK), 1)
    causal = s_idx <= t_idx
    for g in range(GMLP_GROUPS):
        w = jnp.where(causal, ws_ref[g], 0.0).astype(BF16)
        bias = bs_ref[:, g:g + 1]
        cols = slice(g * GROUP_DIM, (g + 1) * GROUP_DIM)
        for c in range(GMLP_ROWS // CHUNK):
            rows = slice(c * CHUNK, (c + 1) * CHUNK)
            mixed = jnp.dot(w, vn[rows, cols], preferred_element_type=F32) + bias
            y_ref[rows, cols] = (u_ref[rows, cols].astype(F32) * mixed).astype(BF16)


def _gmlp_spatial(u, v, ln_g, ln_b, w_s, b_s):
    m, d = u.shape
    tm = GMLP_ROWS
    row = pl.BlockSpec((tm, d), lambda i: (i, 0))
    vec = pl.BlockSpec((1, d), lambda i: (0, 0))
    return pl.pallas_call(
        _gmlp_spatial_kernel,
        grid=(m // tm,),
        in_specs=[row, row, vec, vec,
                  pl.BlockSpec((GMLP_GROUPS, CHUNK, CHUNK), lambda i: (0, 0, 0)),
                  pl.BlockSpec((CHUNK, GMLP_GROUPS), lambda i: (0, 0))],
        out_specs=row,
        out_shape=jax.ShapeDtypeStruct((m, d), BF16),
        compiler_params=_params("parallel"),
        name="gmlp_spatial",
    )(u, v, ln_g[None, :], ln_b[None, :], w_s, b_s.T)


def _mla_down_kernel(x_ref, w_ref, gq_ref, gkv_ref, cos_ref, sin_ref,
                     cq_ref, ckv_ref, kr_ref):
    c = jnp.dot(x_ref[...], w_ref[...], preferred_element_type=F32)
    cq_ref[...] = _rms(c[:, :Q_RANK], gq_ref[...]).astype(BF16)
    ckv_ref[...] = _rms(c[:, Q_RANK:Q_RANK + KV_RANK], gkv_ref[...]).astype(BF16)
    kr = c[:, Q_RANK + KV_RANK:]
    half = ROPE // 2
    swapped = jnp.concatenate([kr[:, half:], kr[:, :half]], axis=-1)
    kr_ref[...] = (kr * cos_ref[:, :ROPE] + swapped * sin_ref[:, :ROPE]).astype(BF16)


def _mla_down(a, w_dqkv, layer, gq, gkv, cos_t, sin_t):
    m, k = a.shape
    n = w_dqkv.shape[2]
    tm = 512
    tab = pl.BlockSpec((tm, LANES), lambda i: (i, 0))
    return pl.pallas_call(
        _mla_down_kernel,
        grid=(m // tm,),
        in_specs=[pl.BlockSpec((tm, k), lambda i: (i, 0)),
                  pl.BlockSpec((None, k, n), lambda i: (layer, 0, 0)),
                  pl.BlockSpec((1, Q_RANK), lambda i: (0, 0)),
                  pl.BlockSpec((1, KV_RANK), lambda i: (0, 0)),
                  tab, tab],
        out_specs=[pl.BlockSpec((tm, Q_RANK), lambda i: (i, 0)),
                   pl.BlockSpec((tm, KV_RANK), lambda i: (i, 0)),
                   pl.BlockSpec((tm, ROPE), lambda i: (i, 0))],
        out_shape=[jax.ShapeDtypeStruct((m, Q_RANK), BF16),
                   jax.ShapeDtypeStruct((m, KV_RANK), BF16),
                   jax.ShapeDtypeStruct((m, ROPE), BF16)],
        compiler_params=_params("parallel"),
        name="mla_down",
    )(a, w_dqkv, gq[None, :], gkv[None, :], cos_t, sin_t)


def _q_rope_kernel(x_ref, w_ref, cos_ref, sin_ref, o_ref):
    r = jnp.dot(x_ref[...], w_ref[...], preferred_element_type=F32)
    cos = cos_ref[...]
    sin = sin_ref[...]
    for t in range(r.shape[1] // LANES):
        x = r[:, t * LANES:(t + 1) * LANES]
        o_ref[:, t * LANES:(t + 1) * LANES] = (
            x * cos + _swap_halves_32(x) * sin).astype(BF16)


def _q_rope(cq, w_rope, layer, cos_t, sin_t):
    m, k = cq.shape
    n = w_rope.shape[2]
    tm, tn = min(1024, m), 1024
    tab = pl.BlockSpec((tm, LANES), lambda i, j: (i, 0))
    return pl.pallas_call(
        _q_rope_kernel,
        grid=(m // tm, n // tn),
        in_specs=[pl.BlockSpec((tm, k), lambda i, j: (i, 0)),
                  _stacked_w_spec(k, tn, layer),
                  tab, tab],
        out_specs=pl.BlockSpec((tm, tn), lambda i, j: (i, j)),
        out_shape=jax.ShapeDtypeStruct((m, n), BF16),
        compiler_params=_params("parallel", "parallel"),
        name="q_rope",
    )(cq, w_rope, cos_t, sin_t)


ATT_TQ = 512
ATT_TK = 512
HEADS_PER_STEP = 2
QK_DIM = NOPE + ROPE
QK_PAD = MXU_WIDTH
EXP2_SCALE = np.float32(QK_DIM ** -0.5 * math.log2(math.e))


def _attention_kernel(qn_ref, qr_ref, kn_ref, kr_ref, v_ref, o_ref,
                      q_scr, k_scr, m_scr, l_scr, acc_scr):
    qi = pl.program_id(2)
    seq = kn_ref.shape[0]

    @pl.when(qi == 0)
    def _():
        for h in range(HEADS_PER_STEP):
            k_scr[h, :, :NOPE] = kn_ref[:, h * NOPE:(h + 1) * NOPE]
            k_scr[h, :, NOPE:QK_DIM] = kr_ref[...]
            k_scr[h, :, QK_DIM:] = jnp.zeros((seq, QK_PAD - QK_DIM), BF16)

    for h in range(HEADS_PER_STEP):
        q_scr[h, :, :NOPE] = qn_ref[:, h * NOPE:(h + 1) * NOPE]
        q_scr[h, :, NOPE:QK_DIM] = qr_ref[:, h * ROPE:(h + 1) * ROPE]
        q_scr[h, :, QK_DIM:] = jnp.zeros((ATT_TQ, QK_PAD - QK_DIM), BF16)
    m_scr[...] = jnp.full(m_scr.shape, -jnp.inf, F32)
    l_scr[...] = jnp.zeros(l_scr.shape, F32)
    acc_scr[...] = jnp.zeros(acc_scr.shape, F32)

    def update(q0, nq, k0, nk, visible_offset):
        q_rows = pl.ds(q0, nq)
        k_rows = pl.ds(k0, nk)
        for h in range(HEADS_PER_STEP):
            s = lax.dot_general(q_scr[h, q_rows, :], k_scr[h, k_rows, :],
                                (((1,), (1,)), ((), ())),
                                preferred_element_type=F32)
            if visible_offset is not None:
                r_idx = lax.broadcasted_iota(jnp.int32, s.shape, 0)
                c_idx = lax.broadcasted_iota(jnp.int32, s.shape, 1)
                s = jnp.where(c_idx < r_idx + visible_offset, s, -jnp.inf)
            m_prev = m_scr[h, q_rows, :]
            m_new = jnp.maximum(m_prev, jnp.max(s, axis=1, keepdims=True))
            m_wide = jnp.concatenate([m_new] * (nk // LANES), axis=1)
            p = jnp.exp2((s - m_wide) * EXP2_SCALE)
            alpha = jnp.exp2((m_prev - m_new) * EXP2_SCALE)
            l_scr[h, q_rows, :] = alpha * l_scr[h, q_rows, :] + jnp.sum(p, axis=1, keepdims=True)
            pv = jnp.dot(p.astype(BF16), v_ref[k_rows, h * V_DIM:(h + 1) * V_DIM],
                         preferred_element_type=F32)
            acc_scr[h, q_rows, :] = alpha * acc_scr[h, q_rows, :] + pv
            m_scr[h, q_rows, :] = m_new

    for n_below in range(seq // ATT_TQ):
        @pl.when(qi == n_below)
        def _(n_below=n_below):
            for j in range(n_below):
                update(0, ATT_TQ, j * ATT_TK, ATT_TK, None)
            update(0, ATT_TQ, n_below * ATT_TK, ATT_TK, 1)

    for h in range(HEADS_PER_STEP):
        o_ref[:, h * V_DIM:(h + 1) * V_DIM] = (acc_scr[h] / l_scr[h]).astype(BF16)


def _attention(q_nope, q_rope, kv, k_rope, batch, seq):
    assert ATT_TQ == ATT_TK and V_DIM == LANES
    m = batch * seq
    nq = seq // ATT_TQ
    hp = HEADS // HEADS_PER_STEP
    wn = HEADS_PER_STEP * NOPE
    wr = HEADS_PER_STEP * ROPE
    wv = HEADS_PER_STEP * V_DIM
    v_off = HEADS * NOPE // wv
    return pl.pallas_call(
        _attention_kernel,
        grid=(batch, hp, nq),
        in_specs=[
            pl.BlockSpec((ATT_TQ, wn), lambda b, p, i: (b * nq + i, p)),
            pl.BlockSpec((ATT_TQ, wr), lambda b, p, i: (b * nq + i, p)),
            pl.BlockSpec((seq, wn), lambda b, p, i: (b, p)),
            pl.BlockSpec((seq, ROPE), lambda b, p, i: (b, 0)),
            pl.BlockSpec((seq, wv), lambda b, p, i: (b, v_off + p)),
        ],
        out_specs=pl.BlockSpec((ATT_TQ, wv), lambda b, p, i: (b * nq + i, p)),
        out_shape=jax.ShapeDtypeStruct((m, HEADS * V_DIM), BF16),
        scratch_shapes=[
            pltpu.VMEM((HEADS_PER_STEP, ATT_TQ, QK_PAD), BF16),
            pltpu.VMEM((HEADS_PER_STEP, seq, QK_PAD), BF16),
            pltpu.VMEM((HEADS_PER_STEP, ATT_TQ, LANES), F32),
            pltpu.VMEM((HEADS_PER_STEP, ATT_TQ, LANES), F32),
            pltpu.VMEM((HEADS_PER_STEP, ATT_TQ, V_DIM), F32),
        ],
        compiler_params=_params("parallel", "parallel", "arbitrary"),
        name="mla_attention",
    )(q_nope, q_rope, kv, k_rope, kv)


def _gmlp_layer(a, j, w_in, ln_g, ln_b, w_s, b_s, w_out):
    u, v = _gmlp_in(a, w_in, j, tm=1024, tn=512)
    y = _gmlp_spatial(u, v, ln_g[j], ln_b[j], w_s[j], b_s[j])
    return _matmul(y, w_out, j, tm=1024, tn=1024, out_dtype=F32, name="gmlp_out")


def _mla_layer(a, j, cos_t, sin_t, w_dqkv, gq, gkv, w_q_nope, w_q_rope, w_kv, w_o,
               batch, seq):
    cq, ckv, k_rope = _mla_down(a, w_dqkv, j, gq[j], gkv[j], cos_t, sin_t)
    q_nope = _matmul(cq, w_q_nope, j, tm=1024, tn=1024, out_dtype=BF16, name="q_nope")
    q_rope = _q_rope(cq, w_q_rope, j, cos_t, sin_t)
    kv = _matmul(ckv, w_kv, j, tm=1024, tn=2048, out_dtype=BF16, name="kv_up")
    o = _attention(q_nope, q_rope, kv, k_rope, batch, seq)
    return _matmul(o, w_o, j, tm=1024, tn=1024, out_dtype=F32, name="attn_out")


def _ffn(a, i, w_gate_up, w_down):
    hid, w_down_bf16 = _swiglu_up(a, w_gate_up, w_down, i, tm=2048)
    return _matmul(hid, w_down_bf16, 0, tm=512, tn=512, out_dtype=F32, name="ffn_down")


def kernel(x, positions, norm_g, gmlp_w_in, gmlp_ln_g, gmlp_ln_b, gmlp_w_s, gmlp_b_s,
           gmlp_w_out, mla_w_dqkv, mla_q_norm_g, mla_kv_norm_g, mla_w_uq, mla_w_ukv,
           mla_w_o, ffn_w_gate_up, ffn_w_down):
    batch, seq, d = x.shape
    n_mla = mla_w_uq.shape[0]
    gmlp_w_in = gmlp_w_in.astype(BF16)
    gmlp_w_out = gmlp_w_out.astype(BF16)
    mla_w_dqkv = mla_w_dqkv.astype(BF16)
    mla_w_o = mla_w_o.astype(BF16)
    w_uq = mla_w_uq.astype(BF16).reshape(n_mla, Q_RANK, HEADS, NOPE + ROPE)
    w_q_nope = w_uq[..., :NOPE].reshape(n_mla, Q_RANK, HEADS * NOPE)
    w_q_rope = w_uq[..., NOPE:].reshape(n_mla, Q_RANK, HEADS * ROPE)
    w_ukv = mla_w_ukv.astype(BF16).reshape(n_mla, KV_RANK, HEADS, NOPE + V_DIM)
    w_kv = jnp.concatenate([w_ukv[..., :NOPE].reshape(n_mla, KV_RANK, HEADS * NOPE),
                            w_ukv[..., NOPE:].reshape(n_mla, KV_RANK, HEADS * V_DIM)], axis=2)

    cos_t, sin_t = _rope_tables(positions)
    h = x.reshape(batch * seq, d)
    a = _prenorm(h, norm_g[0, 0])
    for i in range(DEPTH):
        j = i // 2
        if i % 2 == 0:
            mix = _gmlp_layer(a, j, gmlp_w_in, gmlp_ln_g, gmlp_ln_b, gmlp_w_s, gmlp_b_s,
                              gmlp_w_out)
        else:
            mix = _mla_layer(a, j, cos_t, sin_t, mla_w_dqkv, mla_q_norm_g, mla_kv_norm_g,
                             w_q_nope, w_q_rope, w_kv, mla_w_o, batch, seq)
        h, a = _postnorm(mix, h, norm_g[i, 1], norm_g[i, 2])
        f = _ffn(a, i, ffn_w_gate_up, ffn_w_down)
        g_next = norm_g[i + 1, 0] if i + 1 < DEPTH else None
        h, a = _postnorm(f, h, norm_g[i, 3], g_next)
    return h.reshape(batch, seq, d)
```

```python
import functools
import math

import numpy as np
import jax
import jax.numpy as jnp
from jax import lax
from jax.experimental import pallas as pl
from jax.experimental.pallas import tpu as pltpu

D_MODEL = 4096
DEPTH = 4
CHUNK = 128
GMLP_GROUPS = 32
GROUP_DIM = 128
HEADS = 32
Q_RANK = 1024
KV_RANK = 512
NOPE = 128
ROPE = 64
V_DIM = 128
ROPE_BASE = 10000.0
FFN_HIDDEN = 11008
RMS_EPS = 1e-6
LN_EPS = 1e-5

F32 = jnp.float32
BF16 = jnp.bfloat16

VMEM_LIMIT_BYTES = 58 * 1024 * 1024
LANES = 128
MXU_WIDTH = 256


def _params(*semantics):
    return pltpu.CompilerParams(dimension_semantics=semantics,
                                vmem_limit_bytes=VMEM_LIMIT_BYTES)


def _stacked_w_spec(k, tn, layer, col_block_offset=0):
    return pl.BlockSpec((None, k, tn), lambda i, j: (layer, 0, col_block_offset + j))


def _rms(x, g):
    return x * lax.rsqrt(jnp.mean(x * x, axis=-1, keepdims=True) + RMS_EPS) * g


def _gelu_exact(x):
    return 0.5 * x * (1.0 + lax.erf(x * np.float32(math.sqrt(0.5))))


def _swap_halves_32(x):
    lane = lax.broadcasted_iota(jnp.int32, x.shape, 1)
    first_half = (lane % ROPE) < (ROPE // 2)
    return jnp.where(first_half, pltpu.roll(x, LANES - ROPE // 2, 1),
                     pltpu.roll(x, ROPE // 2, 1))


def _rope_table_kernel(pos_ref, freq_ref, sign_ref, cos_ref, sin_ref):
    ang = pos_ref[...].astype(F32) * freq_ref[...]
    cos_ref[...] = jnp.cos(ang)
    sin_ref[...] = jnp.sin(ang) * sign_ref[...]


def _rope_tables(positions):
    n = positions.size
    tm = min(2048, n)
    inv_freq = ROPE_BASE ** (-jnp.arange(0, ROPE, 2, dtype=F32) / ROPE)
    freq = jnp.tile(inv_freq, LANES // (ROPE // 2))[None, :]
    sign = jnp.tile(jnp.concatenate([-jnp.ones((ROPE // 2,), F32),
                                     jnp.ones((ROPE // 2,), F32)]), LANES // ROPE)[None, :]
    pos = jnp.broadcast_to(positions.reshape(n, 1), (n, LANES))
    row = pl.BlockSpec((tm, LANES), lambda i: (i, 0))
    const = pl.BlockSpec((1, LANES), lambda i: (0, 0))
    return pl.pallas_call(
        _rope_table_kernel,
        grid=(n // tm,),
        in_specs=[row, const, const],
        out_specs=[row, row],
        out_shape=[jax.ShapeDtypeStruct((n, LANES), F32)] * 2,
        compiler_params=_params("parallel"),
        name="rope_tables",
    )(pos, freq, sign)


def _prenorm_kernel(h_ref, g_ref, a_ref):
    a_ref[...] = _rms(h_ref[...], g_ref[...]).astype(BF16)


def _prenorm(h, g):
    m, d = h.shape
    tm = 256
    return pl.pallas_call(
        _prenorm_kernel,
        grid=(m // tm,),
        in_specs=[pl.BlockSpec((tm, d), lambda i: (i, 0)),
                  pl.BlockSpec((1, d), lambda i: (0, 0))],
        out_specs=pl.BlockSpec((tm, d), lambda i: (i, 0)),
        out_shape=jax.ShapeDtypeStruct((m, d), BF16),
        compiler_params=_params("parallel"),
        name="prenorm",
    )(h, g[None, :])


def _postnorm_kernel(f_ref, h_ref, gp_ref, gn_ref, ho_ref, a_ref):
    h_new = h_ref[...] + _rms(f_ref[...], gp_ref[...])
    ho_ref[...] = h_new
    a_ref[...] = _rms(h_new, gn_ref[...]).astype(BF16)


def _postnorm_last_kernel(f_ref, h_ref, gp_ref, ho_ref):
    ho_ref[...] = h_ref[...] + _rms(f_ref[...], gp_ref[...])


def _postnorm(f, h, g_post, g_next):
    m, d = h.shape
    tm = 256
    row = pl.BlockSpec((tm, d), lambda i: (i, 0))
    vec = pl.BlockSpec((1, d), lambda i: (0, 0))
    if g_next is None:
        return pl.pallas_call(
            _postnorm_last_kernel,
            grid=(m // tm,),
            in_specs=[row, row, vec],
            out_specs=row,
            out_shape=jax.ShapeDtypeStruct((m, d), F32),
            compiler_params=_params("parallel"),
            name="postnorm_last",
        )(f, h, g_post[None, :]), None
    return pl.pallas_call(
        _postnorm_kernel,
        grid=(m // tm,),
        in_specs=[row, row, vec, vec],
        out_specs=[row, row],
        out_shape=[jax.ShapeDtypeStruct((m, d), F32),
                   jax.ShapeDtypeStruct((m, d), BF16)],
        compiler_params=_params("parallel"),
        name="postnorm",
    )(f, h, g_post[None, :], g_next[None, :])


def _mm_kernel(x_ref, w_ref, o_ref):
    o_ref[...] = jnp.dot(x_ref[...], w_ref[...],
                         preferred_element_type=F32).astype(o_ref.dtype)


def _matmul(x, w, layer, *, tm, tn, out_dtype, name="matmul"):
    m, k = x.shape
    tm = min(tm, m)
    if layer is None:
        assert w.shape[2] == tn
        n = w.shape[0] * tn
        w_spec = pl.BlockSpec((None, k, tn), lambda i, j: (j, 0, 0))
    else:
        n = w.shape[2]
        w_spec = _stacked_w_spec(k, tn, layer)
    return pl.pallas_call(
        _mm_kernel,
        grid=(m // tm, n // tn),
        in_specs=[pl.BlockSpec((tm, k), lambda i, j: (i, 0)), w_spec],
        out_specs=pl.BlockSpec((tm, tn), lambda i, j: (i, j)),
        out_shape=jax.ShapeDtypeStruct((m, n), out_dtype),
        compiler_params=_params("parallel", "parallel"),
        name=name,
    )(x, w)


def _swiglu_kernel(x_ref, wg_ref, wu_ref, wd_ref, o_ref, wd_bf16_ref):
    x = x_ref[...]
    g = jnp.dot(x, wg_ref[...].astype(BF16), preferred_element_type=F32)
    u = jnp.dot(x, wu_ref[...].astype(BF16), preferred_element_type=F32)
    o_ref[...] = (g * jax.nn.sigmoid(g) * u).astype(o_ref.dtype)
    n_tiles, _, tile = wd_bf16_ref.shape
    for c in range(n_tiles):
        wd_bf16_ref[c] = wd_ref[:, c * tile:(c + 1) * tile].astype(BF16)


def _swiglu_up(x, w_gate_up, w_down, layer, *, tm, down_tn):
    m, k = x.shape
    tm = min(tm, m)
    tn = MXU_WIDTH
    nj = FFN_HIDDEN // tn
    n_steps = (m // tm) * nj
    assert FFN_HIDDEN % n_steps == 0
    slab = FFN_HIDDEN // n_steps
    d = w_down.shape[2]
    n_tiles = d // down_tn
    return pl.pallas_call(
        _swiglu_kernel,
        grid=(m // tm, nj),
        in_specs=[pl.BlockSpec((tm, k), lambda i, j: (i, 0)),
                  _stacked_w_spec(k, tn, layer),
                  _stacked_w_spec(k, tn, layer, nj),
                  pl.BlockSpec((None, slab, d), lambda i, j: (layer, i * nj + j, 0))],
        out_specs=[pl.BlockSpec((tm, tn), lambda i, j: (i, j)),
                   pl.BlockSpec((n_tiles, slab, down_tn), lambda i, j: (0, i * nj + j, 0))],
        out_shape=[jax.ShapeDtypeStruct((m, FFN_HIDDEN), BF16),
                   jax.ShapeDtypeStruct((n_tiles, FFN_HIDDEN, down_tn), BF16)],
        compiler_params=_params("parallel", "parallel"),
        name="swiglu_up",
    )(x, w_gate_up, w_gate_up, w_down)


def _gmlp_in_kernel(x_ref, wu_ref, wv_ref, u_ref, v_ref):
    x = x_ref[...]
    u_ref[...] = _gelu_exact(jnp.dot(x, wu_ref[...], preferred_element_type=F32)).astype(BF16)
    v_ref[...] = _gelu_exact(jnp.dot(x, wv_ref[...], preferred_element_type=F32))


def _gmlp_in(x, w_in, layer, *, tm, tn):
    m, k = x.shape
    tm = min(tm, m)
    nj = D_MODEL // tn
    out = pl.BlockSpec((tm, tn), lambda i, j: (i, j))
    return pl.pallas_call(
        _gmlp_in_kernel,
        grid=(m // tm, nj),
        in_specs=[pl.BlockSpec((tm, k), lambda i, j: (i, 0)),
                  _stacked_w_spec(k, tn, layer),
                  _stacked_w_spec(k, tn, layer, nj)],
        out_specs=[out, out],
        out_shape=[jax.ShapeDtypeStruct((m, D_MODEL), BF16),
                   jax.ShapeDtypeStruct((m, D_MODEL), F32)],
        compiler_params=_params("parallel", "parallel"),
        name="gmlp_in",
    )(x, w_in, w_in)


GMLP_ROWS = 256


def _gmlp_spatial_kernel(u_ref, v_ref, lg_ref, lb_ref, ws_ref, bs_ref, y_ref):
    v = v_ref[...]
    mu = jnp.mean(v, axis=-1, keepdims=True)
    vc = v - mu
    vn = vc * lax.rsqrt(jnp.mean(vc * vc, axis=-1, keepdims=True) + LN_EPS)
    vn = (vn * lg_ref[...] + lb_ref[...]).astype(BF16)
    t_idx = lax.broadcasted_iota(jnp.int32, (CHUNK, CHUNK), 0)
    s_idx = lax.broadcasted_iota(jnp.int32, (CHUNK, CHUNK), 1)
    causal = s_idx <= t_idx
    for g in range(GMLP_GROUPS):
        w = jnp.where(causal, ws_ref[g], 0.0).astype(BF16)
        bias = bs_ref[:, g:g + 1]
        cols = slice(g * GROUP_DIM, (g + 1) * GROUP_DIM)
        for c in range(GMLP_ROWS // CHUNK):
            rows = slice(c * CHUNK, (c + 1) * CHUNK)
            mixed = jnp.dot(w, vn[rows, cols], preferred_element_type=F32) + bias
            y_ref[rows, cols] = (u_ref[rows, cols].astype(F32) * mixed).astype(BF16)


def _gmlp_spatial(u, v, ln_g, ln_b, w_s, b_s):
    m, d = u.shape
    tm = GMLP_ROWS
    row = pl.BlockSpec((tm, d), lambda i: (i, 0))
    vec = pl.BlockSpec((1, d), lambda i: (0, 0))
    return pl.pallas_call(
        _gmlp_spatial_kernel,
        grid=(m // tm,),
        in_specs=[row, row, vec, vec,
                  pl.BlockSpec((GMLP_GROUPS, CHUNK, CHUNK), lambda i: (0, 0, 0)),
                  pl.BlockSpec((CHUNK, GMLP_GROUPS), lambda i: (0, 0))],
        out_specs=row,
        out_shape=jax.ShapeDtypeStruct((m, d), BF16),
        compiler_params=_params("parallel"),
        name="gmlp_spatial",
    )(u, v, ln_g[None, :], ln_b[None, :], w_s, b_s.T)


def _mla_down_kernel(x_ref, w_ref, gq_ref, gkv_ref, cos_ref, sin_ref,
                     cq_ref, ckv_ref, kr_ref):
    c = jnp.dot(x_ref[...], w_ref[...], preferred_element_type=F32)
    cq_ref[...] = _rms(c[:, :Q_RANK], gq_ref[...]).astype(BF16)
    ckv_ref[...] = _rms(c[:, Q_RANK:Q_RANK + KV_RANK], gkv_ref[...]).astype(BF16)
    kr = c[:, Q_RANK + KV_RANK:]
    half = ROPE // 2
    swapped = jnp.concatenate([kr[:, half:], kr[:, :half]], axis=-1)
    kr_ref[...] = (kr * cos_ref[:, :ROPE] + swapped * sin_ref[:, :ROPE]).astype(BF16)


def _mla_down(a, w_dqkv, layer, gq, gkv, cos_t, sin_t):
    m, k = a.shape
    n = w_dqkv.shape[2]
    tm = 512
    tab = pl.BlockSpec((tm, LANES), lambda i: (i, 0))
    return pl.pallas_call(
        _mla_down_kernel,
        grid=(m // tm,),
        in_specs=[pl.BlockSpec((tm, k), lambda i: (i, 0)),
                  pl.BlockSpec((None, k, n), lambda i: (layer, 0, 0)),
                  pl.BlockSpec((1, Q_RANK), lambda i: (0, 0)),
                  pl.BlockSpec((1, KV_RANK), lambda i: (0, 0)),
                  tab, tab],
        out_specs=[pl.BlockSpec((tm, Q_RANK), lambda i: (i, 0)),
                   pl.BlockSpec((tm, KV_RANK), lambda i: (i, 0)),
                   pl.BlockSpec((tm, ROPE), lambda i: (i, 0))],
        out_shape=[jax.ShapeDtypeStruct((m, Q_RANK), BF16),
                   jax.ShapeDtypeStruct((m, KV_RANK), BF16),
                   jax.ShapeDtypeStruct((m, ROPE), BF16)],
        compiler_params=_params("parallel"),
        name="mla_down",
    )(a, w_dqkv, gq[None, :], gkv[None, :], cos_t, sin_t)


def _q_rope_kernel(x_ref, w_ref, cos_ref, sin_ref, o_ref):
    r = jnp.dot(x_ref[...], w_ref[...], preferred_element_type=F32)
    cos = cos_ref[...]
    sin = sin_ref[...]
    for t in range(r.shape[1] // LANES):
        x = r[:, t * LANES:(t + 1) * LANES]
        o_ref[:, t * LANES:(t + 1) * LANES] = (
            x * cos + _swap_halves_32(x) * sin).astype(BF16)


def _q_rope(cq, w_rope, layer, cos_t, sin_t):
    m, k = cq.shape
    n = w_rope.shape[2]
    tm, tn = min(1024, m), 1024
    tab = pl.BlockSpec((tm, LANES), lambda i, j: (i, 0))
    return pl.pallas_call(
        _q_rope_kernel,
        grid=(m // tm, n // tn),
        in_specs=[pl.BlockSpec((tm, k), lambda i, j: (i, 0)),
                  _stacked_w_spec(k, tn, layer),
                  tab, tab],
        out_specs=pl.BlockSpec((tm, tn), lambda i, j: (i, j)),
        out_shape=jax.ShapeDtypeStruct((m, n), BF16),
        compiler_params=_params("parallel", "parallel"),
        name="q_rope",
    )(cq, w_rope, cos_t, sin_t)


ATT_TQ = 512
ATT_TK = 512
HEADS_PER_STEP = 4
QK_DIM = NOPE + ROPE
QK_PAD = MXU_WIDTH
EXP2_SCALE = np.float32(QK_DIM ** -0.5 * math.log2(math.e))


def _attention_kernel(qn_ref, qr_ref, kn_ref, kr_ref, v_ref, o_ref,
                      q_scr, k_scr, m_scr, l_scr, acc_scr):
    qi = pl.program_id(2)
    seq = kn_ref.shape[0]

    @pl.when(qi == 0)
    def _():
        for h in range(HEADS_PER_STEP):
            k_scr[h, :, :NOPE] = kn_ref[:, h * NOPE:(h + 1) * NOPE]
            k_scr[h, :, NOPE:QK_DIM] = kr_ref[...]
            k_scr[h, :, QK_DIM:] = jnp.zeros((seq, QK_PAD - QK_DIM), BF16)

    for h in range(HEADS_PER_STEP):
        q_scr[h, :, :NOPE] = qn_ref[:, h * NOPE:(h + 1) * NOPE]
        q_scr[h, :, NOPE:QK_DIM] = qr_ref[:, h * ROPE:(h + 1) * ROPE]
        q_scr[h, :, QK_DIM:] = jnp.zeros((ATT_TQ, QK_PAD - QK_DIM), BF16)
    m_scr[...] = jnp.full(m_scr.shape, -jnp.inf, F32)
    l_scr[...] = jnp.zeros(l_scr.shape, F32)
    acc_scr[...] = jnp.zeros(acc_scr.shape, F32)

    def update(q0, nq, k0, nk, visible_offset):
        q_rows = pl.ds(q0, nq)
        k_rows = pl.ds(k0, nk)
        for h in range(HEADS_PER_STEP):
            s = lax.dot_general(q_scr[h, q_rows, :], k_scr[h, k_rows, :],
                                (((1,), (1,)), ((), ())),
                                preferred_element_type=F32)
            if visible_offset is not None:
                r_idx = lax.broadcasted_iota(jnp.int32, s.shape, 0)
                c_idx = lax.broadcasted_iota(jnp.int32, s.shape, 1)
                s = jnp.where(c_idx < r_idx + visible_offset, s, -jnp.inf)
            m_prev = m_scr[h, q_rows, :]
            m_new = jnp.maximum(m_prev, jnp.max(s, axis=1, keepdims=True))
            m_wide = jnp.concatenate([m_new] * (nk // LANES), axis=1)
            p = jnp.exp2((s - m_wide) * EXP2_SCALE)
            alpha = jnp.exp2((m_prev - m_new) * EXP2_SCALE)
            l_scr[h, q_rows, :] = alpha * l_scr[h, q_rows, :] + jnp.sum(p, axis=1, keepdims=True)
            pv = jnp.dot(p.astype(BF16), v_ref[k_rows, h * V_DIM:(h + 1) * V_DIM],
                         preferred_element_type=F32)
            acc_scr[h, q_rows, :] = alpha * acc_scr[h, q_rows, :] + pv
            m_scr[h, q_rows, :] = m_new

    for n_below in range(seq // ATT_TQ):
        @pl.when(qi == n_below)
        def _(n_below=n_below):
            for j in range(n_below):
                update(0, ATT_TQ, j * ATT_TK, ATT_TK, None)
            update(0, ATT_TQ, n_below * ATT_TK, ATT_TK, 1)

    for h in range(HEADS_PER_STEP):
        o_ref[:, h * V_DIM:(h + 1) * V_DIM] = (acc_scr[h] / l_scr[h]).astype(BF16)


def _attention(q_nope, q_rope, kv, k_rope, batch, seq):
    assert ATT_TQ == ATT_TK and V_DIM == LANES
    m = batch * seq
    nq = seq // ATT_TQ
    hp = HEADS // HEADS_PER_STEP
    wn = HEADS_PER_STEP * NOPE
    wr = HEADS_PER_STEP * ROPE
    wv = HEADS_PER_STEP * V_DIM
    v_off = HEADS * NOPE // wv
    return pl.pallas_call(
        _attention_kernel,
        grid=(batch, hp, nq),
        in_specs=[
            pl.BlockSpec((ATT_TQ, wn), lambda b, p, i: (b * nq + i, p)),
            pl.BlockSpec((ATT_TQ, wr), lambda b, p, i: (b * nq + i, p)),
            pl.BlockSpec((seq, wn), lambda b, p, i: (b, p)),
            pl.BlockSpec((seq, ROPE), lambda b, p, i: (b, 0)),
            pl.BlockSpec((seq, wv), lambda b, p, i: (b, v_off + p)),
        ],
        out_specs=pl.BlockSpec((ATT_TQ, wv), lambda b, p, i: (b * nq + i, p)),
        out_shape=jax.ShapeDtypeStruct((m, HEADS * V_DIM), BF16),
        scratch_shapes=[
            pltpu.VMEM((HEADS_PER_STEP, ATT_TQ, QK_PAD), BF16),
            pltpu.VMEM((HEADS_PER_STEP, seq, QK_PAD), BF16),
            pltpu.VMEM((HEADS_PER_STEP, ATT_TQ, LANES), F32),
            pltpu.VMEM((HEADS_PER_STEP, ATT_TQ, LANES), F32),
            pltpu.VMEM((HEADS_PER_STEP, ATT_TQ, V_DIM), F32),
        ],
        compiler_params=_params("parallel", "parallel", "arbitrary"),
        name="mla_attention",
    )(q_nope, q_rope, kv, k_rope, kv)


def _gmlp_layer(a, j, w_in, ln_g, ln_b, w_s, b_s, w_out):
    u, v = _gmlp_in(a, w_in, j, tm=1024, tn=512)
    y = _gmlp_spatial(u, v, ln_g[j], ln_b[j], w_s[j], b_s[j])
    return _matmul(y, w_out, j, tm=1024, tn=1024, out_dtype=F32, name="gmlp_out")


def _mla_layer(a, j, cos_t, sin_t, w_dqkv, gq, gkv, w_q_nope, w_q_rope, w_kv, w_o,
               batch, seq):
    cq, ckv, k_rope = _mla_down(a, w_dqkv, j, gq[j], gkv[j], cos_t, sin_t)
    q_nope = _matmul(cq, w_q_nope, j, tm=1024, tn=1024, out_dtype=BF16, name="q_nope")
    q_rope = _q_rope(cq, w_q_rope, j, cos_t, sin_t)
    kv = _matmul(ckv, w_kv, j, tm=1024, tn=2048, out_dtype=BF16, name="kv_up")
    o = _attention(q_nope, q_rope, kv, k_rope, batch, seq)
    return _matmul(o, w_o, j, tm=1024, tn=1024, out_dtype=F32, name="attn_out")


def _ffn(a, i, w_gate_up, w_down):
    down_tn = 512
    hid, w_down_tiles = _swiglu_up(a, w_gate_up, w_down, i, tm=2048, down_tn=down_tn)
    return _matmul(hid, w_down_tiles, None, tm=512, tn=down_tn, out_dtype=F32,
                   name="ffn_down")


def kernel(x, positions, norm_g, gmlp_w_in, gmlp_ln_g, gmlp_ln_b, gmlp_w_s, gmlp_b_s,
           gmlp_w_out, mla_w_dqkv, mla_q_norm_g, mla_kv_norm_g, mla_w_uq, mla_w_ukv,
           mla_w_o, ffn_w_gate_up, ffn_w_down):
    batch, seq, d = x.shape
    n_mla = mla_w_uq.shape[0]
    gmlp_w_in = gmlp_w_in.astype(BF16)
    gmlp_w_out = gmlp_w_out.astype(BF16)
    mla_w_dqkv = mla_w_dqkv.astype(BF16)
    mla_w_o = mla_w_o.astype(BF16)
    w_uq = mla_w_uq.astype(BF16).reshape(n_mla, Q_RANK, HEADS, NOPE + ROPE)
    w_q_nope = w_uq[..., :NOPE].reshape(n_mla, Q_RANK, HEADS * NOPE)
    w_q_rope = w_uq[..., NOPE:].reshape(n_mla, Q_RANK, HEADS * ROPE)
    w_ukv = mla_w_ukv.astype(BF16).reshape(n_mla, KV_RANK, HEADS, NOPE + V_DIM)
    w_kv = jnp.concatenate([w_ukv[..., :NOPE].reshape(n_mla, KV_RANK, HEADS * NOPE),
                            w_ukv[..., NOPE:].reshape(n_mla, KV_RANK, HEADS * V_DIM)], axis=2)

    cos_t, sin_t = _rope_tables(positions)
    h = x.reshape(batch * seq, d)
    a = _prenorm(h, norm_g[0, 0])
    for i in range(DEPTH):
        j = i // 2
        if i % 2 == 0:
            mix = _gmlp_layer(a, j, gmlp_w_in, gmlp_ln_g, gmlp_ln_b, gmlp_w_s, gmlp_b_s,
                              gmlp_w_out)
        else:
            mix = _mla_layer(a, j, cos_t, sin_t, mla_w_dqkv, mla_q_norm_g, mla_kv_norm_g,
                             w_q_nope, w_q_rope, w_kv, mla_w_o, batch, seq)
        h, a = _postnorm(mix, h, norm_g[i, 1], norm_g[i, 2])
        f = _ffn(a, i, ffn_w_gate_up, ffn_w_down)
        g_next = norm_g[i + 1, 0] if i + 1 < DEPTH else None
        h, a = _postnorm(f, h, norm_g[i, 3], g_next)
    return h.reshape(batch, seq, d)
```

```python
import functools
import math

import numpy as np
import jax
import jax.numpy as jnp
from jax import lax
from jax.experimental import pallas as pl
from jax.experimental.pallas import tpu as pltpu

D_MODEL = 4096
DEPTH = 4
CHUNK = 128
GMLP_GROUPS = 32
GROUP_DIM = 128
HEADS = 32
Q_RANK = 1024
KV_RANK = 512
NOPE = 128
ROPE = 64
V_DIM = 128
ROPE_BASE = 10000.0
FFN_HIDDEN = 11008
RMS_EPS = 1e-6
LN_EPS = 1e-5

F32 = jnp.float32
BF16 = jnp.bfloat16

VMEM_LIMIT_BYTES = 58 * 1024 * 1024
LANES = 128
MXU_WIDTH = 256


def _params(*semantics):
    return pltpu.CompilerParams(dimension_semantics=semantics,
                                vmem_limit_bytes=VMEM_LIMIT_BYTES)


def _stacked_w_spec(k, tn, layer, col_block_offset=0):
    return pl.BlockSpec((None, k, tn), lambda i, j: (layer, 0, col_block_offset + j))


def _rms(x, g):
    return x * lax.rsqrt(jnp.mean(x * x, axis=-1, keepdims=True) + RMS_EPS) * g


def _gelu_exact(x):
    return 0.5 * x * (1.0 + lax.erf(x * np.float32(math.sqrt(0.5))))


def _swap_halves_32(x):
    lane = lax.broadcasted_iota(jnp.int32, x.shape, 1)
    first_half = (lane % ROPE) < (ROPE // 2)
    return jnp.where(first_half, pltpu.roll(x, LANES - ROPE // 2, 1),
                     pltpu.roll(x, ROPE // 2, 1))


def _rope_table_kernel(pos_ref, freq_ref, sign_ref, cos_ref, sin_ref):
    ang = pos_ref[...].astype(F32) * freq_ref[...]
    cos_ref[...] = jnp.cos(ang)
    sin_ref[...] = jnp.sin(ang) * sign_ref[...]


def _rope_tables(positions):
    n = positions.size
    tm = min(2048, n)
    inv_freq = ROPE_BASE ** (-jnp.arange(0, ROPE, 2, dtype=F32) / ROPE)
    freq = jnp.tile(inv_freq, LANES // (ROPE // 2))[None, :]
    sign = jnp.tile(jnp.concatenate([-jnp.ones((ROPE // 2,), F32),
                                     jnp.ones((ROPE // 2,), F32)]), LANES // ROPE)[None, :]
    pos = jnp.broadcast_to(positions.reshape(n, 1), (n, LANES))
    row = pl.BlockSpec((tm, LANES), lambda i: (i, 0))
    const = pl.BlockSpec((1, LANES), lambda i: (0, 0))
    return pl.pallas_call(
        _rope_table_kernel,
        grid=(n // tm,),
        in_specs=[row, const, const],
        out_specs=[row, row],
        out_shape=[jax.ShapeDtypeStruct((n, LANES), F32)] * 2,
        compiler_params=_params("parallel"),
        name="rope_tables",
    )(pos, freq, sign)


def _prenorm_kernel(h_ref, g_ref, a_ref):
    a_ref[...] = _rms(h_ref[...], g_ref[...]).astype(BF16)


def _prenorm(h, g):
    m, d = h.shape
    tm = 256
    return pl.pallas_call(
        _prenorm_kernel,
        grid=(m // tm,),
        in_specs=[pl.BlockSpec((tm, d), lambda i: (i, 0)),
                  pl.BlockSpec((1, d), lambda i: (0, 0))],
        out_specs=pl.BlockSpec((tm, d), lambda i: (i, 0)),
        out_shape=jax.ShapeDtypeStruct((m, d), BF16),
        compiler_params=_params("parallel"),
        name="prenorm",
    )(h, g[None, :])


def _postnorm_kernel(f_ref, h_ref, gp_ref, gn_ref, ho_ref, a_ref):
    h_new = h_ref[...] + _rms(f_ref[...], gp_ref[...])
    ho_ref[...] = h_new
    a_ref[...] = _rms(h_new, gn_ref[...]).astype(BF16)


def _postnorm_last_kernel(f_ref, h_ref, gp_ref, ho_ref):
    ho_ref[...] = h_ref[...] + _rms(f_ref[...], gp_ref[...])


def _postnorm(f, h, g_post, g_next):
    m, d = f.shape
    tm = 256
    row = pl.BlockSpec((tm, d), lambda i: (i, 0))
    vec = pl.BlockSpec((1, d), lambda i: (0, 0))
    if g_next is None:
        return pl.pallas_call(
            _postnorm_last_kernel,
            grid=(m // tm,),
            in_specs=[row, row, vec],
            out_specs=row,
            out_shape=jax.ShapeDtypeStruct((m, d), F32),
            compiler_params=_params("parallel"),
            name="postnorm_last",
        )(f, h, g_post[None, :]), None
    return pl.pallas_call(
        _postnorm_kernel,
        grid=(m // tm,),
        in_specs=[row, row, vec, vec],
        out_specs=[row, row],
        out_shape=[jax.ShapeDtypeStruct((m, d), F32),
                   jax.ShapeDtypeStruct((m, d), BF16)],
        compiler_params=_params("parallel"),
        name="postnorm",
    )(f, h, g_post[None, :], g_next[None, :])


def _mm_kernel(x_ref, w_ref, o_ref):
    o_ref[...] = jnp.dot(x_ref[...], w_ref[...],
                         preferred_element_type=F32).astype(o_ref.dtype)


def _matmul(x, w, layer, *, tm, tn, out_dtype, name="matmul"):
    m, k = x.shape
    tm = min(tm, m)
    if layer is None:
        assert w.shape[2] == tn
        n = w.shape[0] * tn
        w_spec = pl.BlockSpec((None, k, tn), lambda i, j: (j, 0, 0))
    else:
        n = w.shape[2]
        w_spec = _stacked_w_spec(k, tn, layer)
    return pl.pallas_call(
        _mm_kernel,
        grid=(m // tm, n // tn),
        in_specs=[pl.BlockSpec((tm, k), lambda i, j: (i, 0)), w_spec],
        out_specs=pl.BlockSpec((tm, tn), lambda i, j: (i, j)),
        out_shape=jax.ShapeDtypeStruct((m, n), out_dtype),
        compiler_params=_params("parallel", "parallel"),
        name=name,
    )(x, w)


def _outproj_postnorm_kernel(x_ref, w_ref, h_ref, gp_ref, gn_ref, ho_ref, a_ref,
                             f_even, f_odd, *, n_row_blocks):
    i = pl.program_id(0)
    j = pl.program_id(1)
    nj, _, _ = f_even.shape
    slab = ho_ref.shape[0]

    def matmul_tile(f_scr):
        f_scr[j] = jnp.dot(x_ref[...], w_ref[...], preferred_element_type=F32)

    def epilogue(f_scr):
        rows = pl.ds(pl.multiple_of(j * slab, slab), slab)
        f = jnp.concatenate([f_scr[c, rows, :] for c in range(nj)], axis=1)
        h_new = h_ref[...] + _rms(f, gp_ref[...])
        ho_ref[...] = h_new
        a_ref[...] = _rms(h_new, gn_ref[...]).astype(BF16)

    @pl.when(i == 0)
    def _():
        matmul_tile(f_even)
        ho_ref[...] = jnp.zeros(ho_ref.shape, F32)
        a_ref[...] = jnp.zeros(a_ref.shape, BF16)

    steady = jnp.logical_and(i > 0, i < n_row_blocks)

    @pl.when(jnp.logical_and(steady, i % 2 == 0))
    def _():
        matmul_tile(f_even)
        epilogue(f_odd)

    @pl.when(jnp.logical_and(steady, i % 2 == 1))
    def _():
        matmul_tile(f_odd)
        epilogue(f_even)

    @pl.when(i == n_row_blocks)
    def _():
        epilogue(f_odd if n_row_blocks % 2 == 0 else f_even)


def _outproj_postnorm(x, w, layer, h, g_post, g_next, *, tm, tn, name):
    m, k = x.shape
    n = w.shape[2]
    tm = min(tm, m)
    nb = m // tm
    nj = n // tn
    slab = tm // nj
    last = nb - 1

    def in_slab(i, j):
        return (jnp.clip(i - 1, 0, last) * nj + j, 0)

    def out_slab(i, j):
        return (jnp.where(i == 0, nb, i - 1) * nj + j, 0)

    vec = pl.BlockSpec((1, n), lambda i, j: (0, 0))
    out_spec = pl.BlockSpec((slab, n), out_slab)
    return pl.pallas_call(
        functools.partial(_outproj_postnorm_kernel, n_row_blocks=nb),
        grid=(nb + 1, nj),
        in_specs=[pl.BlockSpec((tm, k), lambda i, j: (jnp.minimum(i, last), 0)),
                  _stacked_w_spec(k, tn, layer),
                  pl.BlockSpec((slab, n), in_slab), vec, vec],
        out_specs=[out_spec, out_spec],
        out_shape=[jax.ShapeDtypeStruct((m + tm, n), F32),
                   jax.ShapeDtypeStruct((m + tm, n), BF16)],
        scratch_shapes=[pltpu.VMEM((nj, tm, tn), F32),
                        pltpu.VMEM((nj, tm, tn), F32)],
        compiler_params=_params("arbitrary", "arbitrary"),
        name=name,
    )(x, w, h, g_post[None, :], g_next[None, :])


def _swiglu_kernel(x_ref, wg_ref, wu_ref, wd_ref, o_ref, wd_bf16_ref):
    x = x_ref[...]
    g = jnp.dot(x, wg_ref[...].astype(BF16), preferred_element_type=F32)
    u = jnp.dot(x, wu_ref[...].astype(BF16), preferred_element_type=F32)
    o_ref[...] = (g * jax.nn.sigmoid(g) * u).astype(o_ref.dtype)
    n_tiles, _, tile = wd_bf16_ref.shape
    for c in range(n_tiles):
        wd_bf16_ref[c] = wd_ref[:, c * tile:(c + 1) * tile].astype(BF16)


def _swiglu_up(x, w_gate_up, w_down, layer, *, rows, tm, down_tn):
    m, k = rows, x.shape[1]
    tm = min(tm, m)
    tn = MXU_WIDTH
    nj = FFN_HIDDEN // tn
    n_steps = (m // tm) * nj
    assert FFN_HIDDEN % n_steps == 0
    slab = FFN_HIDDEN // n_steps
    d = w_down.shape[2]
    n_tiles = d // down_tn
    return pl.pallas_call(
        _swiglu_kernel,
        grid=(m // tm, nj),
        in_specs=[pl.BlockSpec((tm, k), lambda i, j: (i, 0)),
                  _stacked_w_spec(k, tn, layer),
                  _stacked_w_spec(k, tn, layer, nj),
                  pl.BlockSpec((None, slab, d), lambda i, j: (layer, i * nj + j, 0))],
        out_specs=[pl.BlockSpec((tm, tn), lambda i, j: (i, j)),
                   pl.BlockSpec((n_tiles, slab, down_tn), lambda i, j: (0, i * nj + j, 0))],
        out_shape=[jax.ShapeDtypeStruct((m, FFN_HIDDEN), BF16),
                   jax.ShapeDtypeStruct((n_tiles, FFN_HIDDEN, down_tn), BF16)],
        compiler_params=_params("parallel", "parallel"),
        name="swiglu_up",
    )(x, w_gate_up, w_gate_up, w_down)


def _gmlp_in_kernel(x_ref, wu_ref, wv_ref, u_ref, v_ref):
    x = x_ref[...]
    u_ref[...] = _gelu_exact(jnp.dot(x, wu_ref[...], preferred_element_type=F32)).astype(BF16)
    v_ref[...] = _gelu_exact(jnp.dot(x, wv_ref[...], preferred_element_type=F32))


def _gmlp_in(x, w_in, layer, *, tm, tn):
    m, k = x.shape
    tm = min(tm, m)
    nj = D_MODEL // tn
    out = pl.BlockSpec((tm, tn), lambda i, j: (i, j))
    return pl.pallas_call(
        _gmlp_in_kernel,
        grid=(m // tm, nj),
        in_specs=[pl.BlockSpec((tm, k), lambda i, j: (i, 0)),
                  _stacked_w_spec(k, tn, layer),
                  _stacked_w_spec(k, tn, layer, nj)],
        out_specs=[out, out],
        out_shape=[jax.ShapeDtypeStruct((m, D_MODEL), BF16),
                   jax.ShapeDtypeStruct((m, D_MODEL), F32)],
        compiler_params=_params("parallel", "parallel"),
        name="gmlp_in",
    )(x, w_in, w_in)


GMLP_ROWS = 256


def _gmlp_spatial_kernel(u_ref, v_ref, lg_ref, lb_ref, ws_ref, bs_ref, y_ref):
    v = v_ref[...]
    mu = jnp.mean(v, axis=-1, keepdims=True)
    vc = v - mu
    vn = vc * lax.rsqrt(jnp.mean(vc * vc, axis=-1, keepdims=True) + LN_EPS)
    vn = (vn * lg_ref[...] + lb_ref[...]).astype(BF16)
    t_idx = lax.broadcasted_iota(jnp.int32, (CHUNK, CHUNK), 0)
    s_idx = lax.broadcasted_iota(jnp.int32, (CHUNK, CHUNK), 1)
    causal = s_idx <= t_idx
    for g in range(GMLP_GROUPS):
        w = jnp.where(causal, ws_ref[g], 0.0).astype(BF16)
        bias = bs_ref[:, g:g + 1]
        cols = slice(g * GROUP_DIM, (g + 1) * GROUP_DIM)
        for c in range(GMLP_ROWS // CHUNK):
            rows = slice(c * CHUNK, (c + 1) * CHUNK)
            mixed = jnp.dot(w, vn[rows, cols], preferred_element_type=F32) + bias
            y_ref[rows, cols] = (u_ref[rows, cols].astype(F32) * mixed).astype(BF16)


def _gmlp_spatial(u, v, ln_g, ln_b, w_s, b_s):
    m, d = u.shape
    tm = GMLP_ROWS
    row = pl.BlockSpec((tm, d), lambda i: (i, 0))
    vec = pl.BlockSpec((1, d), lambda i: (0, 0))
    return pl.pallas_call(
        _gmlp_spatial_kernel,
        grid=(m // tm,),
        in_specs=[row, row, vec, vec,
                  pl.BlockSpec((GMLP_GROUPS, CHUNK, CHUNK), lambda i: (0, 0, 0)),
                  pl.BlockSpec((CHUNK, GMLP_GROUPS), lambda i: (0, 0))],
        out_specs=row,
        out_shape=jax.ShapeDtypeStruct((m, d), BF16),
        compiler_params=_params("parallel"),
        name="gmlp_spatial",
    )(u, v, ln_g[None, :], ln_b[None, :], w_s, b_s.T)


def _mla_down_kernel(x_ref, w_ref, gq_ref, gkv_ref, cos_ref, sin_ref,
                     cq_ref, ckv_ref, kr_ref):
    c = jnp.dot(x_ref[...], w_ref[...], preferred_element_type=F32)
    cq_ref[...] = _rms(c[:, :Q_RANK], gq_ref[...]).astype(BF16)
    ckv_ref[...] = _rms(c[:, Q_RANK:Q_RANK + KV_RANK], gkv_ref[...]).astype(BF16)
    kr = c[:, Q_RANK + KV_RANK:]
    half = ROPE // 2
    swapped = jnp.concatenate([kr[:, half:], kr[:, :half]], axis=-1)
    kr_ref[...] = (kr * cos_ref[:, :ROPE] + swapped * sin_ref[:, :ROPE]).astype(BF16)


def _mla_down(a, w_dqkv, layer, gq, gkv, cos_t, sin_t):
    m, k = a.shape
    n = w_dqkv.shape[2]
    tm = 512
    tab = pl.BlockSpec((tm, LANES), lambda i: (i, 0))
    return pl.pallas_call(
        _mla_down_kernel,
        grid=(m // tm,),
        in_specs=[pl.BlockSpec((tm, k), lambda i: (i, 0)),
                  pl.BlockSpec((None, k, n), lambda i: (layer, 0, 0)),
                  pl.BlockSpec((1, Q_RANK), lambda i: (0, 0)),
                  pl.BlockSpec((1, KV_RANK), lambda i: (0, 0)),
                  tab, tab],
        out_specs=[pl.BlockSpec((tm, Q_RANK), lambda i: (i, 0)),
                   pl.BlockSpec((tm, KV_RANK), lambda i: (i, 0)),
                   pl.BlockSpec((tm, ROPE), lambda i: (i, 0))],
        out_shape=[jax.ShapeDtypeStruct((m, Q_RANK), BF16),
                   jax.ShapeDtypeStruct((m, KV_RANK), BF16),
                   jax.ShapeDtypeStruct((m, ROPE), BF16)],
        compiler_params=_params("parallel"),
        name="mla_down",
    )(a, w_dqkv, gq[None, :], gkv[None, :], cos_t, sin_t)


def _q_rope_kernel(x_ref, w_ref, cos_ref, sin_ref, o_ref):
    r = jnp.dot(x_ref[...], w_ref[...], preferred_element_type=F32)
    cos = cos_ref[...]
    sin = sin_ref[...]
    for t in range(r.shape[1] // LANES):
        x = r[:, t * LANES:(t + 1) * LANES]
        o_ref[:, t * LANES:(t + 1) * LANES] = (
            x * cos + _swap_halves_32(x) * sin).astype(BF16)


def _q_rope(cq, w_rope, layer, cos_t, sin_t):
    m, k = cq.shape
    n = w_rope.shape[2]
    tm, tn = min(1024, m), 1024
    tab = pl.BlockSpec((tm, LANES), lambda i, j: (i, 0))
    return pl.pallas_call(
        _q_rope_kernel,
        grid=(m // tm, n // tn),
        in_specs=[pl.BlockSpec((tm, k), lambda i, j: (i, 0)),
                  _stacked_w_spec(k, tn, layer),
                  tab, tab],
        out_specs=pl.BlockSpec((tm, tn), lambda i, j: (i, j)),
        out_shape=jax.ShapeDtypeStruct((m, n), BF16),
        compiler_params=_params("parallel", "parallel"),
        name="q_rope",
    )(cq, w_rope, cos_t, sin_t)


ATT_TQ = 512
ATT_TK = 512
HEADS_PER_STEP = 4
QK_DIM = NOPE + ROPE
QK_PAD = MXU_WIDTH
EXP2_SCALE = np.float32(QK_DIM ** -0.5 * math.log2(math.e))


def _attention_kernel(qn_ref, qr_ref, kn_ref, kr_ref, v_ref, o_ref,
                      q_scr, k_scr, m_scr, l_scr, acc_scr):
    qi = pl.program_id(2)
    seq = kn_ref.shape[0]

    @pl.when(qi == 0)
    def _():
        for h in range(HEADS_PER_STEP):
            k_scr[h, :, :NOPE] = kn_ref[:, h * NOPE:(h + 1) * NOPE]
            k_scr[h, :, NOPE:QK_DIM] = kr_ref[...]
            k_scr[h, :, QK_DIM:] = jnp.zeros((seq, QK_PAD - QK_DIM), BF16)

    for h in range(HEADS_PER_STEP):
        q_scr[h, :, :NOPE] = qn_ref[:, h * NOPE:(h + 1) * NOPE]
        q_scr[h, :, NOPE:QK_DIM] = qr_ref[:, h * ROPE:(h + 1) * ROPE]
        q_scr[h, :, QK_DIM:] = jnp.zeros((ATT_TQ, QK_PAD - QK_DIM), BF16)
    m_scr[...] = jnp.full(m_scr.shape, -jnp.inf, F32)
    l_scr[...] = jnp.zeros(l_scr.shape, F32)
    acc_scr[...] = jnp.zeros(acc_scr.shape, F32)

    def update(q0, nq, k0, nk, visible_offset):
        q_rows = pl.ds(q0, nq)
        k_rows = pl.ds(k0, nk)
        for h in range(HEADS_PER_STEP):
            s = lax.dot_general(q_scr[h, q_rows, :], k_scr[h, k_rows, :],
                                (((1,), (1,)), ((), ())),
                                preferred_element_type=F32)
            if visible_offset is not None:
                r_idx = lax.broadcasted_iota(jnp.int32, s.shape, 0)
                c_idx = lax.broadcasted_iota(jnp.int32, s.shape, 1)
                s = jnp.where(c_idx < r_idx + visible_offset, s, -jnp.inf)
            m_prev = m_scr[h, q_rows, :]
            m_new = jnp.maximum(m_prev, jnp.max(s, axis=1, keepdims=True))
            m_wide = jnp.concatenate([m_new] * (nk // LANES), axis=1)
            p = jnp.exp2((s - m_wide) * EXP2_SCALE)
            alpha = jnp.exp2((m_prev - m_new) * EXP2_SCALE)
            l_scr[h, q_rows, :] = alpha * l_scr[h, q_rows, :] + jnp.sum(p, axis=1, keepdims=True)
            pv = jnp.dot(p.astype(BF16), v_ref[k_rows, h * V_DIM:(h + 1) * V_DIM],
                         preferred_element_type=F32)
            acc_scr[h, q_rows, :] = alpha * acc_scr[h, q_rows, :] + pv
            m_scr[h, q_rows, :] = m_new

    for n_below in range(seq // ATT_TQ):
        @pl.when(qi == n_below)
        def _(n_below=n_below):
            for j in range(n_below):
                update(0, ATT_TQ, j * ATT_TK, ATT_TK, None)
            update(0, ATT_TQ, n_below * ATT_TK, ATT_TK, 1)

    for h in range(HEADS_PER_STEP):
        o_ref[:, h * V_DIM:(h + 1) * V_DIM] = (acc_scr[h] / l_scr[h]).astype(BF16)


def _attention(q_nope, q_rope, kv, k_rope, batch, seq):
    assert ATT_TQ == ATT_TK and V_DIM == LANES
    m = batch * seq
    nq = seq // ATT_TQ
    hp = HEADS // HEADS_PER_STEP
    wn = HEADS_PER_STEP * NOPE
    wr = HEADS_PER_STEP * ROPE
    wv = HEADS_PER_STEP * V_DIM
    v_off = HEADS * NOPE // wv
    return pl.pallas_call(
        _attention_kernel,
        grid=(batch, hp, nq),
        in_specs=[
            pl.BlockSpec((ATT_TQ, wn), lambda b, p, i: (b * nq + i, p)),
            pl.BlockSpec((ATT_TQ, wr), lambda b, p, i: (b * nq + i, p)),
            pl.BlockSpec((seq, wn), lambda b, p, i: (b, p)),
            pl.BlockSpec((seq, ROPE), lambda b, p, i: (b, 0)),
            pl.BlockSpec((seq, wv), lambda b, p, i: (b, v_off + p)),
        ],
        out_specs=pl.BlockSpec((ATT_TQ, wv), lambda b, p, i: (b * nq + i, p)),
        out_shape=jax.ShapeDtypeStruct((m, HEADS * V_DIM), BF16),
        scratch_shapes=[
            pltpu.VMEM((HEADS_PER_STEP, ATT_TQ, QK_PAD), BF16),
            pltpu.VMEM((HEADS_PER_STEP, seq, QK_PAD), BF16),
            pltpu.VMEM((HEADS_PER_STEP, ATT_TQ, LANES), F32),
            pltpu.VMEM((HEADS_PER_STEP, ATT_TQ, LANES), F32),
            pltpu.VMEM((HEADS_PER_STEP, ATT_TQ, V_DIM), F32),
        ],
        compiler_params=_params("parallel", "parallel", "arbitrary"),
        name="mla_attention",
    )(q_nope, q_rope, kv, k_rope, kv)


def _gmlp_layer(a, h, g_post, g_next, j, w_in, ln_g, ln_b, w_s, b_s, w_out):
    u, v = _gmlp_in(a, w_in, j, tm=1024, tn=512)
    y = _gmlp_spatial(u, v, ln_g[j], ln_b[j], w_s[j], b_s[j])
    return _outproj_postnorm(y, w_out, j, h, g_post, g_next, tm=512, tn=1024,
                             name="gmlp_out")


def _mla_layer(a, h, g_post, g_next, j, cos_t, sin_t, w_dqkv, gq, gkv, w_q_nope, w_q_rope,
               w_kv, w_o, batch, seq):
    cq, ckv, k_rope = _mla_down(a, w_dqkv, j, gq[j], gkv[j], cos_t, sin_t)
    q_nope = _matmul(cq, w_q_nope, j, tm=1024, tn=1024, out_dtype=BF16, name="q_nope")
    q_rope = _q_rope(cq, w_q_rope, j, cos_t, sin_t)
    kv = _matmul(ckv, w_kv, j, tm=1024, tn=2048, out_dtype=BF16, name="kv_up")
    o = _attention(q_nope, q_rope, kv, k_rope, batch, seq)
    return _outproj_postnorm(o, w_o, j, h, g_post, g_next, tm=512, tn=1024,
                             name="attn_out")


def _ffn(a, rows, i, w_gate_up, w_down):
    down_tn = 512
    hid, w_down_tiles = _swiglu_up(a, w_gate_up, w_down, i, rows=rows, tm=2048,
                                   down_tn=down_tn)
    return _matmul(hid, w_down_tiles, None, tm=512, tn=down_tn, out_dtype=F32,
                   name="ffn_down")


def kernel(x, positions, norm_g, gmlp_w_in, gmlp_ln_g, gmlp_ln_b, gmlp_w_s, gmlp_b_s,
           gmlp_w_out, mla_w_dqkv, mla_q_norm_g, mla_kv_norm_g, mla_w_uq, mla_w_ukv,
           mla_w_o, ffn_w_gate_up, ffn_w_down):
    batch, seq, d = x.shape
    n_mla = mla_w_uq.shape[0]
    gmlp_w_in = gmlp_w_in.astype(BF16)
    gmlp_w_out = gmlp_w_out.astype(BF16)
    mla_w_dqkv = mla_w_dqkv.astype(BF16)
    mla_w_o = mla_w_o.astype(BF16)
    w_uq = mla_w_uq.astype(BF16).reshape(n_mla, Q_RANK, HEADS, NOPE + ROPE)
    w_q_nope = w_uq[..., :NOPE].reshape(n_mla, Q_RANK, HEADS * NOPE)
    w_q_rope = w_uq[..., NOPE:].reshape(n_mla, Q_RANK, HEADS * ROPE)
    w_ukv = mla_w_ukv.astype(BF16).reshape(n_mla, KV_RANK, HEADS, NOPE + V_DIM)
    w_kv = jnp.concatenate([w_ukv[..., :NOPE].reshape(n_mla, KV_RANK, HEADS * NOPE),
                            w_ukv[..., NOPE:].reshape(n_mla, KV_RANK, HEADS * V_DIM)], axis=2)

    cos_t, sin_t = _rope_tables(positions)
    h = x.reshape(batch * seq, d)
    a = _prenorm(h, norm_g[0, 0])
    for i in range(DEPTH):
        j = i // 2
        if i % 2 == 0:
            h, a = _gmlp_layer(a, h, norm_g[i, 1], norm_g[i, 2], j, gmlp_w_in, gmlp_ln_g,
                               gmlp_ln_b, gmlp_w_s, gmlp_b_s, gmlp_w_out)
        else:
            h, a = _mla_layer(a, h, norm_g[i, 1], norm_g[i, 2], j, cos_t, sin_t, mla_w_dqkv,
                              mla_q_norm_g, mla_kv_norm_g, w_q_nope, w_q_rope, w_kv, mla_w_o,
                              batch, seq)
        f = _ffn(a, batch * seq, i, ffn_w_gate_up, ffn_w_down)
        g_next = norm_g[i + 1, 0] if i + 1 < DEPTH else None
        h, a = _postnorm(f, h, norm_g[i, 3], g_next)
    return h.reshape(batch, seq, d)
```

```python
import functools
import math

import numpy as np
import jax
import jax.numpy as jnp
from jax import lax
from jax.experimental import pallas as pl
from jax.experimental.pallas import tpu as pltpu

D_MODEL = 4096
DEPTH = 4
CHUNK = 128
GMLP_GROUPS = 32
GROUP_DIM = 128
HEADS = 32
Q_RANK = 1024
KV_RANK = 512
NOPE = 128
ROPE = 64
V_DIM = 128
ROPE_BASE = 10000.0
FFN_HIDDEN = 11008
RMS_EPS = 1e-6
LN_EPS = 1e-5

F32 = jnp.float32
BF16 = jnp.bfloat16

VMEM_LIMIT_BYTES = 58 * 1024 * 1024
LANES = 128
MXU_WIDTH = 256


def _params(*semantics):
    return pltpu.CompilerParams(dimension_semantics=semantics,
                                vmem_limit_bytes=VMEM_LIMIT_BYTES)


def _stacked_w_spec(k, tn, layer, col_block_offset=0):
    return pl.BlockSpec((None, k, tn), lambda i, j: (layer, 0, col_block_offset + j))


def _rms(x, g):
    return x * lax.rsqrt(jnp.mean(x * x, axis=-1, keepdims=True) + RMS_EPS) * g


def _gelu_exact(x):
    return 0.5 * x * (1.0 + lax.erf(x * np.float32(math.sqrt(0.5))))


def _swap_halves_32(x):
    lane = lax.broadcasted_iota(jnp.int32, x.shape, 1)
    first_half = (lane % ROPE) < (ROPE // 2)
    return jnp.where(first_half, pltpu.roll(x, LANES - ROPE // 2, 1),
                     pltpu.roll(x, ROPE // 2, 1))


def _rope_table_kernel(pos_ref, freq_ref, sign_ref, cos_ref, sin_ref):
    ang = pos_ref[...].astype(F32) * freq_ref[...]
    cos_ref[...] = jnp.cos(ang)
    sin_ref[...] = jnp.sin(ang) * sign_ref[...]


def _rope_tables(positions):
    n = positions.size
    tm = min(2048, n)
    inv_freq = ROPE_BASE ** (-jnp.arange(0, ROPE, 2, dtype=F32) / ROPE)
    freq = jnp.tile(inv_freq, LANES // (ROPE // 2))[None, :]
    sign = jnp.tile(jnp.concatenate([-jnp.ones((ROPE // 2,), F32),
                                     jnp.ones((ROPE // 2,), F32)]), LANES // ROPE)[None, :]
    pos = jnp.broadcast_to(positions.reshape(n, 1), (n, LANES))
    row = pl.BlockSpec((tm, LANES), lambda i: (i, 0))
    const = pl.BlockSpec((1, LANES), lambda i: (0, 0))
    return pl.pallas_call(
        _rope_table_kernel,
        grid=(n // tm,),
        in_specs=[row, const, const],
        out_specs=[row, row],
        out_shape=[jax.ShapeDtypeStruct((n, LANES), F32)] * 2,
        compiler_params=_params("parallel"),
        name="rope_tables",
    )(pos, freq, sign)


def _prenorm_kernel(h_ref, g_ref, a_ref):
    a_ref[...] = _rms(h_ref[...], g_ref[...]).astype(BF16)


def _prenorm(h, g):
    m, d = h.shape
    tm = 256
    return pl.pallas_call(
        _prenorm_kernel,
        grid=(m // tm,),
        in_specs=[pl.BlockSpec((tm, d), lambda i: (i, 0)),
                  pl.BlockSpec((1, d), lambda i: (0, 0))],
        out_specs=pl.BlockSpec((tm, d), lambda i: (i, 0)),
        out_shape=jax.ShapeDtypeStruct((m, d), BF16),
        compiler_params=_params("parallel"),
        name="prenorm",
    )(h, g[None, :])


def _postnorm_kernel(f_ref, h_ref, gp_ref, gn_ref, ho_ref, a_ref):
    h_new = h_ref[...] + _rms(f_ref[...], gp_ref[...])
    ho_ref[...] = h_new
    a_ref[...] = _rms(h_new, gn_ref[...]).astype(BF16)


def _postnorm_last_kernel(f_ref, h_ref, gp_ref, ho_ref):
    ho_ref[...] = h_ref[...] + _rms(f_ref[...], gp_ref[...])


def _postnorm(f, h, g_post, g_next):
    m, d = f.shape
    tm = 256
    row = pl.BlockSpec((tm, d), lambda i: (i, 0))
    vec = pl.BlockSpec((1, d), lambda i: (0, 0))
    if g_next is None:
        return pl.pallas_call(
            _postnorm_last_kernel,
            grid=(m // tm,),
            in_specs=[row, row, vec],
            out_specs=row,
            out_shape=jax.ShapeDtypeStruct((m, d), F32),
            compiler_params=_params("parallel"),
            name="postnorm_last",
        )(f, h, g_post[None, :]), None
    return pl.pallas_call(
        _postnorm_kernel,
        grid=(m // tm,),
        in_specs=[row, row, vec, vec],
        out_specs=[row, row],
        out_shape=[jax.ShapeDtypeStruct((m, d), F32),
                   jax.ShapeDtypeStruct((m, d), BF16)],
        compiler_params=_params("parallel"),
        name="postnorm",
    )(f, h, g_post[None, :], g_next[None, :])


def _mm_kernel(x_ref, w_ref, o_ref):
    o_ref[...] = jnp.dot(x_ref[...], w_ref[...],
                         preferred_element_type=F32).astype(o_ref.dtype)


def _matmul(x, w, layer, *, tm, tn, out_dtype, name="matmul"):
    m, k = x.shape
    tm = min(tm, m)
    if layer is None:
        assert w.shape[2] == tn
        n = w.shape[0] * tn
        w_spec = pl.BlockSpec((None, k, tn), lambda i, j: (j, 0, 0))
    else:
        n = w.shape[2]
        w_spec = _stacked_w_spec(k, tn, layer)
    return pl.pallas_call(
        _mm_kernel,
        grid=(m // tm, n // tn),
        in_specs=[pl.BlockSpec((tm, k), lambda i, j: (i, 0)), w_spec],
        out_specs=pl.BlockSpec((tm, tn), lambda i, j: (i, j)),
        out_shape=jax.ShapeDtypeStruct((m, n), out_dtype),
        compiler_params=_params("parallel", "parallel"),
        name=name,
    )(x, w)


def _outproj_postnorm_kernel(x_ref, w_ref, h_ref, gp_ref, gn_ref, ho_ref, a_ref,
                             f_even, f_odd, *, n_row_blocks):
    i = pl.program_id(0)
    j = pl.program_id(1)
    nj, _, _ = f_even.shape
    slab = ho_ref.shape[0]

    def matmul_tile(f_scr):
        f_scr[j] = jnp.dot(x_ref[...], w_ref[...], preferred_element_type=F32)

    def epilogue(f_scr):
        rows = pl.ds(pl.multiple_of(j * slab, slab), slab)
        f = jnp.concatenate([f_scr[c, rows, :] for c in range(nj)], axis=1)
        h_new = h_ref[...] + _rms(f, gp_ref[...])
        ho_ref[...] = h_new
        a_ref[...] = _rms(h_new, gn_ref[...]).astype(BF16)

    @pl.when(i == 0)
    def _():
        matmul_tile(f_even)
        ho_ref[...] = jnp.zeros(ho_ref.shape, F32)
        a_ref[...] = jnp.zeros(a_ref.shape, BF16)

    steady = jnp.logical_and(i > 0, i < n_row_blocks)

    @pl.when(jnp.logical_and(steady, i % 2 == 0))
    def _():
        matmul_tile(f_even)
        epilogue(f_odd)

    @pl.when(jnp.logical_and(steady, i % 2 == 1))
    def _():
        matmul_tile(f_odd)
        epilogue(f_even)

    @pl.when(i == n_row_blocks)
    def _():
        epilogue(f_odd if n_row_blocks % 2 == 0 else f_even)


def _outproj_postnorm(x, w, layer, h, g_post, g_next, *, tm, tn, name):
    m, k = x.shape
    n = w.shape[2]
    tm = min(tm, m)
    nb = m // tm
    nj = n // tn
    slab = tm // nj
    last = nb - 1

    def in_slab(i, j):
        return (jnp.clip(i - 1, 0, last) * nj + j, 0)

    def out_slab(i, j):
        return (jnp.where(i == 0, nb, i - 1) * nj + j, 0)

    vec = pl.BlockSpec((1, n), lambda i, j: (0, 0))
    out_spec = pl.BlockSpec((slab, n), out_slab)
    return pl.pallas_call(
        functools.partial(_outproj_postnorm_kernel, n_row_blocks=nb),
        grid=(nb + 1, nj),
        in_specs=[pl.BlockSpec((tm, k), lambda i, j: (jnp.minimum(i, last), 0)),
                  _stacked_w_spec(k, tn, layer),
                  pl.BlockSpec((slab, n), in_slab), vec, vec],
        out_specs=[out_spec, out_spec],
        out_shape=[jax.ShapeDtypeStruct((m + tm, n), F32),
                   jax.ShapeDtypeStruct((m + tm, n), BF16)],
        scratch_shapes=[pltpu.VMEM((nj, tm, tn), F32),
                        pltpu.VMEM((nj, tm, tn), F32)],
        compiler_params=_params("arbitrary", "arbitrary"),
        name=name,
    )(x, w, h, g_post[None, :], g_next[None, :])


def _swiglu_kernel(x_ref, wg_ref, wu_ref, wd_ref, o_ref, wd_bf16_ref):
    x = x_ref[...]
    g = jnp.dot(x, wg_ref[...].astype(BF16), preferred_element_type=F32)
    u = jnp.dot(x, wu_ref[...].astype(BF16), preferred_element_type=F32)
    o_ref[...] = (g * jax.nn.sigmoid(g) * u).astype(o_ref.dtype)
    n_tiles, _, tile = wd_bf16_ref.shape
    for c in range(n_tiles):
        wd_bf16_ref[c] = wd_ref[:, c * tile:(c + 1) * tile].astype(BF16)


def _swiglu_up(x, w_gate_up, w_down, layer, *, rows, tm, down_tn):
    m, k = rows, x.shape[1]
    tm = min(tm, m)
    tn = MXU_WIDTH
    nj = FFN_HIDDEN // tn
    n_steps = (m // tm) * nj
    assert FFN_HIDDEN % n_steps == 0
    slab = FFN_HIDDEN // n_steps
    d = w_down.shape[2]
    n_tiles = d // down_tn
    return pl.pallas_call(
        _swiglu_kernel,
        grid=(m // tm, nj),
        in_specs=[pl.BlockSpec((tm, k), lambda i, j: (i, 0)),
                  _stacked_w_spec(k, tn, layer),
                  _stacked_w_spec(k, tn, layer, nj),
                  pl.BlockSpec((None, slab, d), lambda i, j: (layer, i * nj + j, 0))],
        out_specs=[pl.BlockSpec((tm, tn), lambda i, j: (i, j)),
                   pl.BlockSpec((n_tiles, slab, down_tn), lambda i, j: (0, i * nj + j, 0))],
        out_shape=[jax.ShapeDtypeStruct((m, FFN_HIDDEN), BF16),
                   jax.ShapeDtypeStruct((n_tiles, FFN_HIDDEN, down_tn), BF16)],
        compiler_params=_params("parallel", "parallel"),
        name="swiglu_up",
    )(x, w_gate_up, w_gate_up, w_down)


def _gmlp_in_kernel(x_ref, wu_ref, wv_ref, u_ref, v_ref):
    x = x_ref[...]
    u_ref[...] = _gelu_exact(jnp.dot(x, wu_ref[...], preferred_element_type=F32)).astype(BF16)
    v_ref[...] = _gelu_exact(jnp.dot(x, wv_ref[...], preferred_element_type=F32))


def _gmlp_in(x, w_in, layer, *, tm, tn):
    m, k = x.shape
    tm = min(tm, m)
    nj = D_MODEL // tn
    out = pl.BlockSpec((tm, tn), lambda i, j: (i, j))
    return pl.pallas_call(
        _gmlp_in_kernel,
        grid=(m // tm, nj),
        in_specs=[pl.BlockSpec((tm, k), lambda i, j: (i, 0)),
                  _stacked_w_spec(k, tn, layer),
                  _stacked_w_spec(k, tn, layer, nj)],
        out_specs=[out, out],
        out_shape=[jax.ShapeDtypeStruct((m, D_MODEL), BF16),
                   jax.ShapeDtypeStruct((m, D_MODEL), F32)],
        compiler_params=_params("parallel", "parallel"),
        name="gmlp_in",
    )(x, w_in, w_in)


GMLP_ROWS = 256


def _gmlp_spatial_kernel(u_ref, v_ref, lg_ref, lb_ref, ws_ref, bs_ref, y_ref):
    v = v_ref[...]
    mu = jnp.mean(v, axis=-1, keepdims=True)
    vc = v - mu
    vn = vc * lax.rsqrt(jnp.mean(vc * vc, axis=-1, keepdims=True) + LN_EPS)
    vn = (vn * lg_ref[...] + lb_ref[...]).astype(BF16)
    t_idx = lax.broadcasted_iota(jnp.int32, (CHUNK, CHUNK), 0)
    s_idx = lax.broadcasted_iota(jnp.int32, (CHUNK, CHUNK), 1)
    causal = s_idx <= t_idx
    for g in range(GMLP_GROUPS):
        w = jnp.where(causal, ws_ref[g], 0.0).astype(BF16)
        bias = bs_ref[:, g:g + 1]
        cols = slice(g * GROUP_DIM, (g + 1) * GROUP_DIM)
        for c in range(GMLP_ROWS // CHUNK):
            rows = slice(c * CHUNK, (c + 1) * CHUNK)
            mixed = jnp.dot(w, vn[rows, cols], preferred_element_type=F32) + bias
            y_ref[rows, cols] = (u_ref[rows, cols].astype(F32) * mixed).astype(BF16)


def _gmlp_spatial(u, v, ln_g, ln_b, w_s, b_s):
    m, d = u.shape
    tm = GMLP_ROWS
    row = pl.BlockSpec((tm, d), lambda i: (i, 0))
    vec = pl.BlockSpec((1, d), lambda i: (0, 0))
    return pl.pallas_call(
        _gmlp_spatial_kernel,
        grid=(m // tm,),
        in_specs=[row, row, vec, vec,
                  pl.BlockSpec((GMLP_GROUPS, CHUNK, CHUNK), lambda i: (0, 0, 0)),
                  pl.BlockSpec((CHUNK, GMLP_GROUPS), lambda i: (0, 0))],
        out_specs=row,
        out_shape=jax.ShapeDtypeStruct((m, d), BF16),
        compiler_params=_params("parallel"),
        name="gmlp_spatial",
    )(u, v, ln_g[None, :], ln_b[None, :], w_s, b_s.T)


def _mla_down_kernel(x_ref, w_ref, gq_ref, gkv_ref, cos_ref, sin_ref,
                     cq_ref, ckv_ref, kr_ref):
    c = jnp.dot(x_ref[...], w_ref[...], preferred_element_type=F32)
    cq_ref[...] = _rms(c[:, :Q_RANK], gq_ref[...]).astype(BF16)
    ckv_ref[...] = _rms(c[:, Q_RANK:Q_RANK + KV_RANK], gkv_ref[...]).astype(BF16)
    kr = c[:, Q_RANK + KV_RANK:]
    half = ROPE // 2
    swapped = jnp.concatenate([kr[:, half:], kr[:, :half]], axis=-1)
    kr_ref[...] = (kr * cos_ref[:, :ROPE] + swapped * sin_ref[:, :ROPE]).astype(BF16)


def _mla_down(a, w_dqkv, layer, gq, gkv, cos_t, sin_t):
    m, k = a.shape
    n = w_dqkv.shape[2]
    tm = 512
    tab = pl.BlockSpec((tm, LANES), lambda i: (i, 0))
    return pl.pallas_call(
        _mla_down_kernel,
        grid=(m // tm,),
        in_specs=[pl.BlockSpec((tm, k), lambda i: (i, 0)),
                  pl.BlockSpec((None, k, n), lambda i: (layer, 0, 0)),
                  pl.BlockSpec((1, Q_RANK), lambda i: (0, 0)),
                  pl.BlockSpec((1, KV_RANK), lambda i: (0, 0)),
                  tab, tab],
        out_specs=[pl.BlockSpec((tm, Q_RANK), lambda i: (i, 0)),
                   pl.BlockSpec((tm, KV_RANK), lambda i: (i, 0)),
                   pl.BlockSpec((tm, ROPE), lambda i: (i, 0))],
        out_shape=[jax.ShapeDtypeStruct((m, Q_RANK), BF16),
                   jax.ShapeDtypeStruct((m, KV_RANK), BF16),
                   jax.ShapeDtypeStruct((m, ROPE), BF16)],
        compiler_params=_params("parallel"),
        name="mla_down",
    )(a, w_dqkv, gq[None, :], gkv[None, :], cos_t, sin_t)


def _q_rope_kernel(x_ref, w_ref, cos_ref, sin_ref, o_ref):
    r = jnp.dot(x_ref[...], w_ref[...], preferred_element_type=F32)
    cos = cos_ref[...]
    sin = sin_ref[...]
    for t in range(r.shape[1] // LANES):
        x = r[:, t * LANES:(t + 1) * LANES]
        o_ref[:, t * LANES:(t + 1) * LANES] = (
            x * cos + _swap_halves_32(x) * sin).astype(BF16)


def _q_rope(cq, w_rope, layer, cos_t, sin_t):
    m, k = cq.shape
    n = w_rope.shape[2]
    tm, tn = min(1024, m), 1024
    tab = pl.BlockSpec((tm, LANES), lambda i, j: (i, 0))
    return pl.pallas_call(
        _q_rope_kernel,
        grid=(m // tm, n // tn),
        in_specs=[pl.BlockSpec((tm, k), lambda i, j: (i, 0)),
                  _stacked_w_spec(k, tn, layer),
                  tab, tab],
        out_specs=pl.BlockSpec((tm, tn), lambda i, j: (i, j)),
        out_shape=jax.ShapeDtypeStruct((m, n), BF16),
        compiler_params=_params("parallel", "parallel"),
        name="q_rope",
    )(cq, w_rope, cos_t, sin_t)


ATT_TQ = 512
ATT_TK = 512
HEADS_PER_STEP = 4
QK_DIM = NOPE + ROPE
QK_PAD = MXU_WIDTH
EXP2_SCALE = np.float32(QK_DIM ** -0.5 * math.log2(math.e))


def _attention_kernel(qn_ref, qr_ref, kn_ref, kr_ref, v_ref, o_ref,
                      q_scr, k_scr, m_scr, l_scr, acc_scr):
    qi = pl.program_id(2)
    seq = kn_ref.shape[0]

    @pl.when(qi == 0)
    def _():
        for h in range(HEADS_PER_STEP):
            k_scr[h, :, :NOPE] = kn_ref[:, h * NOPE:(h + 1) * NOPE]
            k_scr[h, :, NOPE:QK_DIM] = kr_ref[...]
            k_scr[h, :, QK_DIM:] = jnp.zeros((seq, QK_PAD - QK_DIM), BF16)
            q_scr[h, :, QK_DIM:] = jnp.zeros((ATT_TQ, QK_PAD - QK_DIM), BF16)

    for h in range(HEADS_PER_STEP):
        q_scr[h, :, :NOPE] = qn_ref[:, h * NOPE:(h + 1) * NOPE]
        q_scr[h, :, NOPE:QK_DIM] = qr_ref[:, h * ROPE:(h + 1) * ROPE]
    def key_block(j, on_diagonal):
        rows = pl.ds(j * ATT_TK, ATT_TK)
        first = j == 0
        for h in range(HEADS_PER_STEP):
            s = lax.dot_general(q_scr[h], k_scr[h, rows, :], (((1,), (1,)), ((), ())),
                                preferred_element_type=F32)
            if on_diagonal:
                r_idx = lax.broadcasted_iota(jnp.int32, s.shape, 0)
                c_idx = lax.broadcasted_iota(jnp.int32, s.shape, 1)
                s = jnp.where(c_idx <= r_idx, s, -jnp.inf)
            m_new = jnp.broadcast_to(jnp.max(s, axis=1, keepdims=True), (ATT_TQ, LANES))
            if not first:
                m_prev = m_scr[h]
                m_new = jnp.maximum(m_prev, m_new)
            m_wide = jnp.concatenate([m_new] * (ATT_TK // LANES), axis=1)
            p = jnp.exp2((s - m_wide) * EXP2_SCALE)
            l_new = jnp.broadcast_to(jnp.sum(p, axis=1, keepdims=True), (ATT_TQ, LANES))
            acc_new = jnp.dot(p.astype(BF16), v_ref[rows, h * V_DIM:(h + 1) * V_DIM],
                              preferred_element_type=F32)
            if not first:
                alpha = jnp.exp2((m_prev - m_new) * EXP2_SCALE)
                l_new = alpha * l_scr[h] + l_new
                acc_new = alpha * acc_scr[h] + acc_new
            l_scr[h] = l_new
            acc_scr[h] = acc_new
            m_scr[h] = m_new

    for n_below in range(seq // ATT_TQ):
        @pl.when(qi == n_below)
        def _(n_below=n_below):
            for j in range(n_below):
                key_block(j, False)
            key_block(n_below, True)

    for h in range(HEADS_PER_STEP):
        o_ref[:, h * V_DIM:(h + 1) * V_DIM] = (acc_scr[h] / l_scr[h]).astype(BF16)


def _attention(q_nope, q_rope, kv, k_rope, batch, seq):
    assert ATT_TQ == ATT_TK and V_DIM == LANES
    m = batch * seq
    nq = seq // ATT_TQ
    hp = HEADS // HEADS_PER_STEP
    wn = HEADS_PER_STEP * NOPE
    wr = HEADS_PER_STEP * ROPE
    wv = HEADS_PER_STEP * V_DIM
    v_off = HEADS * NOPE // wv
    return pl.pallas_call(
        _attention_kernel,
        grid=(batch, hp, nq),
        in_specs=[
            pl.BlockSpec((ATT_TQ, wn), lambda b, p, i: (b * nq + i, p)),
            pl.BlockSpec((ATT_TQ, wr), lambda b, p, i: (b * nq + i, p)),
            pl.BlockSpec((seq, wn), lambda b, p, i: (b, p)),
            pl.BlockSpec((seq, ROPE), lambda b, p, i: (b, 0)),
            pl.BlockSpec((seq, wv), lambda b, p, i: (b, v_off + p)),
        ],
        out_specs=pl.BlockSpec((ATT_TQ, wv), lambda b, p, i: (b * nq + i, p)),
        out_shape=jax.ShapeDtypeStruct((m, HEADS * V_DIM), BF16),
        scratch_shapes=[
            pltpu.VMEM((HEADS_PER_STEP, ATT_TQ, QK_PAD), BF16),
            pltpu.VMEM((HEADS_PER_STEP, seq, QK_PAD), BF16),
            pltpu.VMEM((HEADS_PER_STEP, ATT_TQ, LANES), F32),
            pltpu.VMEM((HEADS_PER_STEP, ATT_TQ, LANES), F32),
            pltpu.VMEM((HEADS_PER_STEP, ATT_TQ, V_DIM), F32),
        ],
        compiler_params=_params("parallel", "parallel", "arbitrary"),
        name="mla_attention",
    )(q_nope, q_rope, kv, k_rope, kv)


def _gmlp_layer(a, h, g_post, g_next, j, w_in, ln_g, ln_b, w_s, b_s, w_out):
    u, v = _gmlp_in(a, w_in, j, tm=1024, tn=512)
    y = _gmlp_spatial(u, v, ln_g[j], ln_b[j], w_s[j], b_s[j])
    return _outproj_postnorm(y, w_out, j, h, g_post, g_next, tm=512, tn=1024,
                             name="gmlp_out")


def _mla_layer(a, h, g_post, g_next, j, cos_t, sin_t, w_dqkv, gq, gkv, w_q_nope, w_q_rope,
               w_kv, w_o, batch, seq):
    cq, ckv, k_rope = _mla_down(a, w_dqkv, j, gq[j], gkv[j], cos_t, sin_t)
    q_nope = _matmul(cq, w_q_nope, j, tm=1024, tn=1024, out_dtype=BF16, name="q_nope")
    q_rope = _q_rope(cq, w_q_rope, j, cos_t, sin_t)
    kv = _matmul(ckv, w_kv, j, tm=1024, tn=2048, out_dtype=BF16, name="kv_up")
    o = _attention(q_nope, q_rope, kv, k_rope, batch, seq)
    return _outproj_postnorm(o, w_o, j, h, g_post, g_next, tm=512, tn=1024,
                             name="attn_out")


def _ffn(a, rows, i, w_gate_up, w_down):
    down_tn = 512
    hid, w_down_tiles = _swiglu_up(a, w_gate_up, w_down, i, rows=rows, tm=2048,
                                   down_tn=down_tn)
    return _matmul(hid, w_down_tiles, None, tm=512, tn=down_tn, out_dtype=F32,
                   name="ffn_down")


def kernel(x, positions, norm_g, gmlp_w_in, gmlp_ln_g, gmlp_ln_b, gmlp_w_s, gmlp_b_s,
           gmlp_w_out, mla_w_dqkv, mla_q_norm_g, mla_kv_norm_g, mla_w_uq, mla_w_ukv,
           mla_w_o, ffn_w_gate_up, ffn_w_down):
    batch, seq, d = x.shape
    n_mla = mla_w_uq.shape[0]
    gmlp_w_in = gmlp_w_in.astype(BF16)
    gmlp_w_out = gmlp_w_out.astype(BF16)
    mla_w_dqkv = mla_w_dqkv.astype(BF16)
    mla_w_o = mla_w_o.astype(BF16)
    w_uq = mla_w_uq.astype(BF16).reshape(n_mla, Q_RANK, HEADS, NOPE + ROPE)
    w_q_nope = w_uq[..., :NOPE].reshape(n_mla, Q_RANK, HEADS * NOPE)
    w_q_rope = w_uq[..., NOPE:].reshape(n_mla, Q_RANK, HEADS * ROPE)
    w_ukv = mla_w_ukv.astype(BF16).reshape(n_mla, KV_RANK, HEADS, NOPE + V_DIM)
    w_kv = jnp.concatenate([w_ukv[..., :NOPE].reshape(n_mla, KV_RANK, HEADS * NOPE),
                            w_ukv[..., NOPE:].reshape(n_mla, KV_RANK, HEADS * V_DIM)], axis=2)

    cos_t, sin_t = _rope_tables(positions)
    h = x.reshape(batch * seq, d)
    a = _prenorm(h, norm_g[0, 0])
    for i in range(DEPTH):
        j = i // 2
        if i % 2 == 0:
            h, a = _gmlp_layer(a, h, norm_g[i, 1], norm_g[i, 2], j, gmlp_w_in, gmlp_ln_g,
                               gmlp_ln_b, gmlp_w_s, gmlp_b_s, gmlp_w_out)
        else:
            h, a = _mla_layer(a, h, norm_g[i, 1], norm_g[i, 2], j, cos_t, sin_t, mla_w_dqkv,
                              mla_q_norm_g, mla_kv_norm_g, w_q_nope, w_q_rope, w_kv, mla_w_o,
                              batch, seq)
        f = _ffn(a, batch * seq, i, ffn_w_gate_up, ffn_w_down)
        g_next = norm_g[i + 1, 0] if i + 1 < DEPTH else None
        h, a = _postnorm(f, h, norm_g[i, 3], g_next)
    return h.reshape(batch, seq, d)
```

```python
import functools
import math

import numpy as np
import jax
import jax.numpy as jnp
from jax import lax
from jax.experimental import pallas as pl
from jax.experimental.pallas import tpu as pltpu

D_MODEL = 4096
DEPTH = 4
CHUNK = 128
GMLP_GROUPS = 32
GROUP_DIM = 128
HEADS = 32
Q_RANK = 1024
KV_RANK = 512
NOPE = 128
ROPE = 64
V_DIM = 128
ROPE_BASE = 10000.0
FFN_HIDDEN = 11008
RMS_EPS = 1e-6
LN_EPS = 1e-5

F32 = jnp.float32
BF16 = jnp.bfloat16

VMEM_LIMIT_BYTES = 60000 * 1024
LANES = 128
MXU_WIDTH = 256


def _params(*semantics):
    return pltpu.CompilerParams(dimension_semantics=semantics,
                                vmem_limit_bytes=VMEM_LIMIT_BYTES)


def _stacked_w_spec(k, tn, layer, col_block_offset=0):
    return pl.BlockSpec((None, k, tn), lambda i, j: (layer, 0, col_block_offset + j))


def _rms(x, g):
    return x * lax.rsqrt(jnp.mean(x * x, axis=-1, keepdims=True) + RMS_EPS) * g


def _gelu_exact(x):
    return 0.5 * x * (1.0 + lax.erf(x * np.float32(math.sqrt(0.5))))


def _swap_halves_32(x):
    lane = lax.broadcasted_iota(jnp.int32, x.shape, 1)
    first_half = (lane % ROPE) < (ROPE // 2)
    return jnp.where(first_half, pltpu.roll(x, LANES - ROPE // 2, 1),
                     pltpu.roll(x, ROPE // 2, 1))


def _rope_table_kernel(pos_ref, freq_ref, sign_ref, cos_ref, sin_ref):
    ang = pos_ref[...].astype(F32) * freq_ref[...]
    cos_ref[...] = jnp.cos(ang)
    sin_ref[...] = jnp.sin(ang) * sign_ref[...]


def _rope_tables(positions):
    n = positions.size
    tm = min(2048, n)
    inv_freq = ROPE_BASE ** (-jnp.arange(0, ROPE, 2, dtype=F32) / ROPE)
    freq = jnp.tile(inv_freq, LANES // (ROPE // 2))[None, :]
    sign = jnp.tile(jnp.concatenate([-jnp.ones((ROPE // 2,), F32),
                                     jnp.ones((ROPE // 2,), F32)]), LANES // ROPE)[None, :]
    pos = jnp.broadcast_to(positions.reshape(n, 1), (n, LANES))
    row = pl.BlockSpec((tm, LANES), lambda i: (i, 0))
    const = pl.BlockSpec((1, LANES), lambda i: (0, 0))
    return pl.pallas_call(
        _rope_table_kernel,
        grid=(n // tm,),
        in_specs=[row, const, const],
        out_specs=[row, row],
        out_shape=[jax.ShapeDtypeStruct((n, LANES), F32)] * 2,
        compiler_params=_params("parallel"),
        name="rope_tables",
    )(pos, freq, sign)


def _postnorm_kernel(f_ref, h_ref, gp_ref, gn_ref, ho_ref, a_ref):
    h_new = h_ref[...] + _rms(f_ref[...], gp_ref[...])
    ho_ref[...] = h_new
    a_ref[...] = _rms(h_new, gn_ref[...]).astype(BF16)


def _postnorm_last_kernel(f_ref, h_ref, gp_ref, ho_ref):
    ho_ref[...] = h_ref[...] + _rms(f_ref[...], gp_ref[...])


def _postnorm(f, h, g_post, g_next):
    m, d = f.shape
    tm = 256
    row = pl.BlockSpec((tm, d), lambda i: (i, 0))
    vec = pl.BlockSpec((1, d), lambda i: (0, 0))
    if g_next is None:
        return pl.pallas_call(
            _postnorm_last_kernel,
            grid=(m // tm,),
            in_specs=[row, row, vec],
            out_specs=row,
            out_shape=jax.ShapeDtypeStruct((m, d), F32),
            compiler_params=_params("parallel"),
            name="postnorm_last",
        )(f, h, g_post[None, :]), None
    return pl.pallas_call(
        _postnorm_kernel,
        grid=(m // tm,),
        in_specs=[row, row, vec, vec],
        out_specs=[row, row],
        out_shape=[jax.ShapeDtypeStruct((m, d), F32),
                   jax.ShapeDtypeStruct((m, d), BF16)],
        compiler_params=_params("parallel"),
        name="postnorm",
    )(f, h, g_post[None, :], g_next[None, :])


def _mm_kernel(x_ref, w_ref, o_ref):
    o_ref[...] = jnp.dot(x_ref[...], w_ref[...],
                         preferred_element_type=F32).astype(o_ref.dtype)


def _matmul(x, w, layer, *, tm, tn, out_dtype, name="matmul"):
    m, k = x.shape
    tm = min(tm, m)
    if layer is None:
        assert w.shape[2] == tn
        n = w.shape[0] * tn
        w_spec = pl.BlockSpec((None, k, tn), lambda i, j: (j, 0, 0))
    else:
        n = w.shape[2]
        w_spec = _stacked_w_spec(k, tn, layer)
    return pl.pallas_call(
        _mm_kernel,
        grid=(m // tm, n // tn),
        in_specs=[pl.BlockSpec((tm, k), lambda i, j: (i, 0)), w_spec],
        out_specs=pl.BlockSpec((tm, tn), lambda i, j: (i, j)),
        out_shape=jax.ShapeDtypeStruct((m, n), out_dtype),
        compiler_params=_params("parallel", "parallel"),
        name=name,
    )(x, w)


def _outproj_postnorm_kernel(x_ref, w_ref, h_ref, gp_ref, gn_ref, ho_ref, a_ref,
                             f_even, f_odd, *, n_row_blocks):
    i = pl.program_id(0)
    j = pl.program_id(1)
    nj, _, _ = f_even.shape
    slab = ho_ref.shape[0]

    def matmul_tile(f_scr):
        f_scr[j] = jnp.dot(x_ref[...], w_ref[...], preferred_element_type=F32)

    def epilogue(f_scr):
        rows = pl.ds(pl.multiple_of(j * slab, slab), slab)
        f = jnp.concatenate([f_scr[c, rows, :] for c in range(nj)], axis=1)
        h_new = h_ref[...] + _rms(f, gp_ref[...])
        ho_ref[...] = h_new
        a_ref[...] = _rms(h_new, gn_ref[...]).astype(BF16)

    @pl.when(i == 0)
    def _():
        matmul_tile(f_even)
        ho_ref[...] = jnp.zeros(ho_ref.shape, F32)
        a_ref[...] = jnp.zeros(a_ref.shape, BF16)

    steady = jnp.logical_and(i > 0, i < n_row_blocks)

    @pl.when(jnp.logical_and(steady, i % 2 == 0))
    def _():
        matmul_tile(f_even)
        epilogue(f_odd)

    @pl.when(jnp.logical_and(steady, i % 2 == 1))
    def _():
        matmul_tile(f_odd)
        epilogue(f_even)

    @pl.when(i == n_row_blocks)
    def _():
        epilogue(f_odd if n_row_blocks % 2 == 0 else f_even)


def _outproj_postnorm(x, w, layer, h, g_post, g_next, *, tm, tn, name):
    m, k = x.shape
    n = w.shape[2]
    tm = min(tm, m)
    nb = m // tm
    nj = n // tn
    slab = tm // nj
    last = nb - 1

    def in_slab(i, j):
        return (jnp.clip(i - 1, 0, last) * nj + j, 0)

    def out_slab(i, j):
        return (jnp.where(i == 0, nb, i - 1) * nj + j, 0)

    vec = pl.BlockSpec((1, n), lambda i, j: (0, 0))
    out_spec = pl.BlockSpec((slab, n), out_slab)
    return pl.pallas_call(
        functools.partial(_outproj_postnorm_kernel, n_row_blocks=nb),
        grid=(nb + 1, nj),
        in_specs=[pl.BlockSpec((tm, k), lambda i, j: (jnp.minimum(i, last), 0)),
                  _stacked_w_spec(k, tn, layer),
                  pl.BlockSpec((slab, n), in_slab), vec, vec],
        out_specs=[out_spec, out_spec],
        out_shape=[jax.ShapeDtypeStruct((m + tm, n), F32),
                   jax.ShapeDtypeStruct((m + tm, n), BF16)],
        scratch_shapes=[pltpu.VMEM((nj, tm, tn), F32),
                        pltpu.VMEM((nj, tm, tn), F32)],
        compiler_params=_params("arbitrary", "arbitrary"),
        name=name,
    )(x, w, h, g_post[None, :], g_next[None, :])


def _swiglu_kernel(x_ref, wg_ref, wu_ref, wd_ref, o_ref, wd_bf16_ref):
    x = x_ref[...]
    g = jnp.dot(x, wg_ref[...].astype(BF16), preferred_element_type=F32)
    u = jnp.dot(x, wu_ref[...].astype(BF16), preferred_element_type=F32)
    o_ref[...] = (g * jax.nn.sigmoid(g) * u).astype(o_ref.dtype)
    n_tiles, _, tile = wd_bf16_ref.shape
    for c in range(n_tiles):
        wd_bf16_ref[c] = wd_ref[:, c * tile:(c + 1) * tile].astype(BF16)


def _swiglu_up(x, w_gate_up, w_down, layer, *, rows, tm, down_tn):
    m, k = rows, x.shape[1]
    tm = min(tm, m)
    tn = MXU_WIDTH
    nj = FFN_HIDDEN // tn
    n_steps = (m // tm) * nj
    assert FFN_HIDDEN % n_steps == 0
    slab = FFN_HIDDEN // n_steps
    d = w_down.shape[2]
    n_tiles = d // down_tn
    return pl.pallas_call(
        _swiglu_kernel,
        grid=(m // tm, nj),
        in_specs=[pl.BlockSpec((tm, k), lambda i, j: (i, 0)),
                  _stacked_w_spec(k, tn, layer),
                  _stacked_w_spec(k, tn, layer, nj),
                  pl.BlockSpec((None, slab, d), lambda i, j: (layer, i * nj + j, 0))],
        out_specs=[pl.BlockSpec((tm, tn), lambda i, j: (i, j)),
                   pl.BlockSpec((n_tiles, slab, down_tn), lambda i, j: (0, i * nj + j, 0))],
        out_shape=[jax.ShapeDtypeStruct((m, FFN_HIDDEN), BF16),
                   jax.ShapeDtypeStruct((n_tiles, FFN_HIDDEN, down_tn), BF16)],
        compiler_params=_params("parallel", "parallel"),
        name="swiglu_up",
    )(x, w_gate_up, w_gate_up, w_down)


def _gmlp_in_kernel(*refs, n_row_blocks, has_residual):
    if has_residual:
        (f_ref, h_ref, gp_ref, gn_ref, wu_ref, wv_ref,
         ho_ref, u_ref, v_ref, a_even, a_odd) = refs
    else:
        h_ref, gn_ref, wu_ref, wv_ref, u_ref, v_ref, a_even, a_odd = refs
    i = pl.program_id(0)
    j = pl.program_id(1)
    slab = h_ref.shape[0]

    def norm_slab(a_scr):
        rows = pl.ds(pl.multiple_of(j * slab, slab), slab)
        h = h_ref[...]
        if has_residual:
            h = h + _rms(f_ref[...], gp_ref[...])
            ho_ref[...] = h
        a_scr[rows, :] = _rms(h, gn_ref[...]).astype(BF16)

    def matmul_tile(a_scr):
        a = a_scr[...]
        u_ref[...] = _gelu_exact(
            jnp.dot(a, wu_ref[...], preferred_element_type=F32)).astype(BF16)
        v_ref[...] = _gelu_exact(jnp.dot(a, wv_ref[...], preferred_element_type=F32))

    @pl.when(i == 0)
    def _():
        norm_slab(a_even)
        u_ref[...] = jnp.zeros(u_ref.shape, BF16)
        v_ref[...] = jnp.zeros(v_ref.shape, F32)

    steady = jnp.logical_and(i > 0, i < n_row_blocks)

    @pl.when(jnp.logical_and(steady, i % 2 == 0))
    def _():
        norm_slab(a_even)
        matmul_tile(a_odd)

    @pl.when(jnp.logical_and(steady, i % 2 == 1))
    def _():
        norm_slab(a_odd)
        matmul_tile(a_even)

    @pl.when(i == n_row_blocks)
    def _():
        matmul_tile(a_odd if n_row_blocks % 2 == 0 else a_even)
        if has_residual:
            ho_ref[...] = jnp.zeros(ho_ref.shape, F32)


def _gmlp_in(h, f, g_post, g_pre, w_in, layer, *, rows, tm, tn):
    m, k = rows, h.shape[1]
    tm = min(tm, m)
    nb = m // tm
    nj = D_MODEL // tn
    slab = tm // nj
    last = nb - 1
    has_residual = f is not None

    slab_in = pl.BlockSpec((slab, k), lambda i, j: (jnp.minimum(i, last) * nj + j, 0))
    slab_out = pl.BlockSpec((slab, k), lambda i, j: (i * nj + j, 0))
    vec = pl.BlockSpec((1, k), lambda i, j: (0, 0))
    tile_out = pl.BlockSpec((tm, tn), lambda i, j: (jnp.where(i == 0, nb, i - 1), j))
    w_specs = [_stacked_w_spec(k, tn, layer), _stacked_w_spec(k, tn, layer, nj)]
    padded = m + tm
    uv_shapes = [jax.ShapeDtypeStruct((padded, D_MODEL), BF16),
                 jax.ShapeDtypeStruct((padded, D_MODEL), F32)]
    if has_residual:
        in_specs = [slab_in, slab_in, vec, vec] + w_specs
        out_specs = [slab_out, tile_out, tile_out]
        out_shape = [jax.ShapeDtypeStruct((padded, k), F32)] + uv_shapes
        args = (f, h, g_post[None, :], g_pre[None, :], w_in, w_in)
    else:
        in_specs = [slab_in, vec] + w_specs
        out_specs = [tile_out, tile_out]
        out_shape = uv_shapes
        args = (h, g_pre[None, :], w_in, w_in)
    return pl.pallas_call(
        functools.partial(_gmlp_in_kernel, n_row_blocks=nb, has_residual=has_residual),
        grid=(nb + 1, nj),
        in_specs=in_specs,
        out_specs=out_specs,
        out_shape=out_shape,
        scratch_shapes=[pltpu.VMEM((tm, k), BF16), pltpu.VMEM((tm, k), BF16)],
        compiler_params=_params("arbitrary", "arbitrary"),
        name="gmlp_in",
    )(*args)


GMLP_ROWS = 256


def _gmlp_spatial_kernel(u_ref, v_ref, lg_ref, lb_ref, ws_ref, bs_ref, y_ref):
    v = v_ref[...]
    mu = jnp.mean(v, axis=-1, keepdims=True)
    vc = v - mu
    vn = vc * lax.rsqrt(jnp.mean(vc * vc, axis=-1, keepdims=True) + LN_EPS)
    vn = (vn * lg_ref[...] + lb_ref[...]).astype(BF16)
    t_idx = lax.broadcasted_iota(jnp.int32, (CHUNK, CHUNK), 0)
    s_idx = lax.broadcasted_iota(jnp.int32, (CHUNK, CHUNK), 1)
    causal = s_idx <= t_idx
    for g in range(GMLP_GROUPS):
        w = jnp.where(causal, ws_ref[g], 0.0).astype(BF16)
        bias = bs_ref[:, g:g + 1]
        cols = slice(g * GROUP_DIM, (g + 1) * GROUP_DIM)
        for c in range(GMLP_ROWS // CHUNK):
            rows = slice(c * CHUNK, (c + 1) * CHUNK)
            mixed = jnp.dot(w, vn[rows, cols], preferred_element_type=F32) + bias
            y_ref[rows, cols] = (u_ref[rows, cols].astype(F32) * mixed).astype(BF16)


def _gmlp_spatial(u, v, ln_g, ln_b, w_s, b_s, *, rows):
    m, d = rows, u.shape[1]
    tm = GMLP_ROWS
    row = pl.BlockSpec((tm, d), lambda i: (i, 0))
    vec = pl.BlockSpec((1, d), lambda i: (0, 0))
    return pl.pallas_call(
        _gmlp_spatial_kernel,
        grid=(m // tm,),
        in_specs=[row, row, vec, vec,
                  pl.BlockSpec((GMLP_GROUPS, CHUNK, CHUNK), lambda i: (0, 0, 0)),
                  pl.BlockSpec((CHUNK, GMLP_GROUPS), lambda i: (0, 0))],
        out_specs=row,
        out_shape=jax.ShapeDtypeStruct((m, d), BF16),
        compiler_params=_params("parallel"),
        name="gmlp_spatial",
    )(u, v, ln_g[None, :], ln_b[None, :], w_s, b_s.T)


def _mla_down_kernel(x_ref, w_ref, gq_ref, gkv_ref, cos_ref, sin_ref,
                     cq_ref, ckv_ref, kr_ref):
    c = jnp.dot(x_ref[...], w_ref[...], preferred_element_type=F32)
    cq_ref[...] = _rms(c[:, :Q_RANK], gq_ref[...]).astype(BF16)
    ckv_ref[...] = _rms(c[:, Q_RANK:Q_RANK + KV_RANK], gkv_ref[...]).astype(BF16)
    kr = c[:, Q_RANK + KV_RANK:]
    half = ROPE // 2
    swapped = jnp.concatenate([kr[:, half:], kr[:, :half]], axis=-1)
    kr_ref[...] = (kr * cos_ref[:, :ROPE] + swapped * sin_ref[:, :ROPE]).astype(BF16)


def _mla_down(a, w_dqkv, layer, gq, gkv, cos_t, sin_t):
    m, k = a.shape
    n = w_dqkv.shape[2]
    tm = 512
    tab = pl.BlockSpec((tm, LANES), lambda i: (i, 0))
    return pl.pallas_call(
        _mla_down_kernel,
        grid=(m // tm,),
        in_specs=[pl.BlockSpec((tm, k), lambda i: (i, 0)),
                  pl.BlockSpec((None, k, n), lambda i: (layer, 0, 0)),
                  pl.BlockSpec((1, Q_RANK), lambda i: (0, 0)),
                  pl.BlockSpec((1, KV_RANK), lambda i: (0, 0)),
                  tab, tab],
        out_specs=[pl.BlockSpec((tm, Q_RANK), lambda i: (i, 0)),
                   pl.BlockSpec((tm, KV_RANK), lambda i: (i, 0)),
                   pl.BlockSpec((tm, ROPE), lambda i: (i, 0))],
        out_shape=[jax.ShapeDtypeStruct((m, Q_RANK), BF16),
                   jax.ShapeDtypeStruct((m, KV_RANK), BF16),
                   jax.ShapeDtypeStruct((m, ROPE), BF16)],
        compiler_params=_params("parallel"),
        name="mla_down",
    )(a, w_dqkv, gq[None, :], gkv[None, :], cos_t, sin_t)


def _q_rope_kernel(x_ref, w_ref, cos_ref, sin_ref, o_ref):
    r = jnp.dot(x_ref[...], w_ref[...], preferred_element_type=F32)
    cos = cos_ref[...]
    sin = sin_ref[...]
    for t in range(r.shape[1] // LANES):
        x = r[:, t * LANES:(t + 1) * LANES]
        o_ref[:, t * LANES:(t + 1) * LANES] = (
            x * cos + _swap_halves_32(x) * sin).astype(BF16)


def _q_rope(cq, w_rope, layer, cos_t, sin_t):
    m, k = cq.shape
    n = w_rope.shape[2]
    tm, tn = min(1024, m), 1024
    tab = pl.BlockSpec((tm, LANES), lambda i, j: (i, 0))
    return pl.pallas_call(
        _q_rope_kernel,
        grid=(m // tm, n // tn),
        in_specs=[pl.BlockSpec((tm, k), lambda i, j: (i, 0)),
                  _stacked_w_spec(k, tn, layer),
                  tab, tab],
        out_specs=pl.BlockSpec((tm, tn), lambda i, j: (i, j)),
        out_shape=jax.ShapeDtypeStruct((m, n), BF16),
        compiler_params=_params("parallel", "parallel"),
        name="q_rope",
    )(cq, w_rope, cos_t, sin_t)


ATT_TQ = 512
ATT_TK = 512
HEADS_PER_STEP = 4
QK_DIM = NOPE + ROPE
QK_PAD = MXU_WIDTH
EXP2_SCALE = np.float32(QK_DIM ** -0.5 * math.log2(math.e))


def _attention_kernel(qn_ref, qr_ref, kn_ref, kr_ref, v_ref, o_ref,
                      q_scr, k_scr, m_scr, l_scr, acc_scr):
    qi = pl.program_id(2)
    seq = kn_ref.shape[0]

    @pl.when(qi == 0)
    def _():
        for h in range(HEADS_PER_STEP):
            k_scr[h, :, :NOPE] = kn_ref[:, h * NOPE:(h + 1) * NOPE]
            k_scr[h, :, NOPE:QK_DIM] = kr_ref[...]
            k_scr[h, :, QK_DIM:] = jnp.zeros((seq, QK_PAD - QK_DIM), BF16)
            q_scr[h, :, QK_DIM:] = jnp.zeros((ATT_TQ, QK_PAD - QK_DIM), BF16)

    for h in range(HEADS_PER_STEP):
        q_scr[h, :, :NOPE] = qn_ref[:, h * NOPE:(h + 1) * NOPE]
        q_scr[h, :, NOPE:QK_DIM] = qr_ref[:, h * ROPE:(h + 1) * ROPE]
    def key_block(j, on_diagonal):
        rows = pl.ds(j * ATT_TK, ATT_TK)
        first = j == 0
        for h in range(HEADS_PER_STEP):
            s = lax.dot_general(q_scr[h], k_scr[h, rows, :], (((1,), (1,)), ((), ())),
                                preferred_element_type=F32)
            if on_diagonal:
                r_idx = lax.broadcasted_iota(jnp.int32, s.shape, 0)
                c_idx = lax.broadcasted_iota(jnp.int32, s.shape, 1)
                s = jnp.where(c_idx <= r_idx, s, -jnp.inf)
            m_new = jnp.broadcast_to(jnp.max(s, axis=1, keepdims=True), (ATT_TQ, LANES))
            if not first:
                m_prev = m_scr[h]
                m_new = jnp.maximum(m_prev, m_new)
            m_wide = jnp.concatenate([m_new] * (ATT_TK // LANES), axis=1)
            p = jnp.exp2((s - m_wide) * EXP2_SCALE)
            l_new = jnp.broadcast_to(jnp.sum(p, axis=1, keepdims=True), (ATT_TQ, LANES))
            acc_new = jnp.dot(p.astype(BF16), v_ref[rows, h * V_DIM:(h + 1) * V_DIM],
                              preferred_element_type=F32)
            if not first:
                alpha = jnp.exp2((m_prev - m_new) * EXP2_SCALE)
                l_new = alpha * l_scr[h] + l_new
                acc_new = alpha * acc_scr[h] + acc_new
            l_scr[h] = l_new
            acc_scr[h] = acc_new
            m_scr[h] = m_new

    for n_below in range(seq // ATT_TQ):
        @pl.when(qi == n_below)
        def _(n_below=n_below):
            for j in range(n_below):
                key_block(j, False)
            key_block(n_below, True)

    for h in range(HEADS_PER_STEP):
        o_ref[:, h * V_DIM:(h + 1) * V_DIM] = (acc_scr[h] / l_scr[h]).astype(BF16)


def _attention(q_nope, q_rope, kv, k_rope, batch, seq):
    assert ATT_TQ == ATT_TK and V_DIM == LANES
    m = batch * seq
    nq = seq // ATT_TQ
    hp = HEADS // HEADS_PER_STEP
    wn = HEADS_PER_STEP * NOPE
    wr = HEADS_PER_STEP * ROPE
    wv = HEADS_PER_STEP * V_DIM
    v_off = HEADS * NOPE // wv
    return pl.pallas_call(
        _attention_kernel,
        grid=(batch, hp, nq),
        in_specs=[
            pl.BlockSpec((ATT_TQ, wn), lambda b, p, i: (b * nq + i, p)),
            pl.BlockSpec((ATT_TQ, wr), lambda b, p, i: (b * nq + i, p)),
            pl.BlockSpec((seq, wn), lambda b, p, i: (b, p)),
            pl.BlockSpec((seq, ROPE), lambda b, p, i: (b, 0)),
            pl.BlockSpec((seq, wv), lambda b, p, i: (b, v_off + p)),
        ],
        out_specs=pl.BlockSpec((ATT_TQ, wv), lambda b, p, i: (b * nq + i, p)),
        out_shape=jax.ShapeDtypeStruct((m, HEADS * V_DIM), BF16),
        scratch_shapes=[
            pltpu.VMEM((HEADS_PER_STEP, ATT_TQ, QK_PAD), BF16),
            pltpu.VMEM((HEADS_PER_STEP, seq, QK_PAD), BF16),
            pltpu.VMEM((HEADS_PER_STEP, ATT_TQ, LANES), F32),
            pltpu.VMEM((HEADS_PER_STEP, ATT_TQ, LANES), F32),
            pltpu.VMEM((HEADS_PER_STEP, ATT_TQ, V_DIM), F32),
        ],
        compiler_params=_params("parallel", "parallel", "arbitrary"),
        name="mla_attention",
    )(q_nope, q_rope, kv, k_rope, kv)


def _gmlp_layer(h, pending, g_pre, g_post, g_next, j, rows, w_in, ln_g, ln_b, w_s, b_s,
                w_out):
    if pending is None:
        u, v = _gmlp_in(h, None, None, g_pre, w_in, j, rows=rows, tm=1024, tn=512)
    else:
        f, g_prev_post = pending
        h, u, v = _gmlp_in(h, f, g_prev_post, g_pre, w_in, j, rows=rows, tm=1024, tn=512)
    y = _gmlp_spatial(u, v, ln_g[j], ln_b[j], w_s[j], b_s[j], rows=rows)
    return _outproj_postnorm(y, w_out, j, h, g_post, g_next, tm=512, tn=1024,
                             name="gmlp_out")


def _mla_layer(a, h, g_post, g_next, j, cos_t, sin_t, w_dqkv, gq, gkv, w_q_nope, w_q_rope,
               w_kv, w_o, batch, seq):
    cq, ckv, k_rope = _mla_down(a, w_dqkv, j, gq[j], gkv[j], cos_t, sin_t)
    q_nope = _matmul(cq, w_q_nope, j, tm=1024, tn=1024, out_dtype=BF16, name="q_nope")
    q_rope = _q_rope(cq, w_q_rope, j, cos_t, sin_t)
    kv = _matmul(ckv, w_kv, j, tm=1024, tn=2048, out_dtype=BF16, name="kv_up")
    o = _attention(q_nope, q_rope, kv, k_rope, batch, seq)
    return _outproj_postnorm(o, w_o, j, h, g_post, g_next, tm=512, tn=1024,
                             name="attn_out")


def _ffn(a, rows, i, w_gate_up, w_down):
    down_tn = 512
    hid, w_down_tiles = _swiglu_up(a, w_gate_up, w_down, i, rows=rows, tm=2048,
                                   down_tn=down_tn)
    return _matmul(hid, w_down_tiles, None, tm=512, tn=down_tn, out_dtype=F32,
                   name="ffn_down")


def kernel(x, positions, norm_g, gmlp_w_in, gmlp_ln_g, gmlp_ln_b, gmlp_w_s, gmlp_b_s,
           gmlp_w_out, mla_w_dqkv, mla_q_norm_g, mla_kv_norm_g, mla_w_uq, mla_w_ukv,
           mla_w_o, ffn_w_gate_up, ffn_w_down):
    batch, seq, d = x.shape
    n_mla = mla_w_uq.shape[0]
    gmlp_w_in = gmlp_w_in.astype(BF16)
    gmlp_w_out = gmlp_w_out.astype(BF16)
    mla_w_dqkv = mla_w_dqkv.astype(BF16)
    mla_w_o = mla_w_o.astype(BF16)
    w_uq = mla_w_uq.astype(BF16).reshape(n_mla, Q_RANK, HEADS, NOPE + ROPE)
    w_q_nope = w_uq[..., :NOPE].reshape(n_mla, Q_RANK, HEADS * NOPE)
    w_q_rope = w_uq[..., NOPE:].reshape(n_mla, Q_RANK, HEADS * ROPE)
    w_ukv = mla_w_ukv.astype(BF16).reshape(n_mla, KV_RANK, HEADS, NOPE + V_DIM)
    w_kv = jnp.concatenate([w_ukv[..., :NOPE].reshape(n_mla, KV_RANK, HEADS * NOPE),
                            w_ukv[..., NOPE:].reshape(n_mla, KV_RANK, HEADS * V_DIM)], axis=2)

    cos_t, sin_t = _rope_tables(positions)
    rows = batch * seq
    h = x.reshape(rows, d)
    a = None
    pending = None
    for i in range(DEPTH):
        j = i // 2
        if i % 2 == 0:
            h, a = _gmlp_layer(h, pending, norm_g[i, 0], norm_g[i, 1], norm_g[i, 2], j, rows,
                               gmlp_w_in, gmlp_ln_g, gmlp_ln_b, gmlp_w_s, gmlp_b_s,
                               gmlp_w_out)
            pending = None
        else:
            h, a = _mla_layer(a, h, norm_g[i, 1], norm_g[i, 2], j, cos_t, sin_t, mla_w_dqkv,
                              mla_q_norm_g, mla_kv_norm_g, w_q_nope, w_q_rope, w_kv, mla_w_o,
                              batch, seq)
        f = _ffn(a, rows, i, ffn_w_gate_up, ffn_w_down)
        if i + 1 < DEPTH and (i + 1) % 2 == 0:
            pending, a = (f, norm_g[i, 3]), None
        else:
            g_next = norm_g[i + 1, 0] if i + 1 < DEPTH else None
            h, a = _postnorm(f, h, norm_g[i, 3], g_next)
    return h.reshape(batch, seq, d)
```

```python
import functools
import math

import numpy as np
import jax
import jax.numpy as jnp
from jax import lax
from jax.experimental import pallas as pl
from jax.experimental.pallas import tpu as pltpu

D_MODEL = 4096
DEPTH = 4
CHUNK = 128
GMLP_GROUPS = 32
GROUP_DIM = 128
HEADS = 32
Q_RANK = 1024
KV_RANK = 512
NOPE = 128
ROPE = 64
V_DIM = 128
ROPE_BASE = 10000.0
FFN_HIDDEN = 11008
RMS_EPS = 1e-6
LN_EPS = 1e-5

F32 = jnp.float32
BF16 = jnp.bfloat16

VMEM_LIMIT_BYTES = 60000 * 1024
LANES = 128
MXU_WIDTH = 256


def _params(*semantics):
    return pltpu.CompilerParams(dimension_semantics=semantics,
                                vmem_limit_bytes=VMEM_LIMIT_BYTES)


def _stacked_w_spec(k, tn, layer, col_block_offset=0):
    return pl.BlockSpec((None, k, tn), lambda i, j: (layer, 0, col_block_offset + j))


def _rms(x, g):
    return x * lax.rsqrt(jnp.mean(x * x, axis=-1, keepdims=True) + RMS_EPS) * g


def _gelu_exact(x):
    return 0.5 * x * (1.0 + lax.erf(x * np.float32(math.sqrt(0.5))))


def _swap_halves_32(x):
    lane = lax.broadcasted_iota(jnp.int32, x.shape, 1)
    first_half = (lane % ROPE) < (ROPE // 2)
    return jnp.where(first_half, pltpu.roll(x, LANES - ROPE // 2, 1),
                     pltpu.roll(x, ROPE // 2, 1))


def _rope_table_kernel(pos_ref, freq_ref, sign_ref, cos_ref, sin_ref):
    ang = pos_ref[...].astype(F32) * freq_ref[...]
    cos_ref[...] = jnp.cos(ang)
    sin_ref[...] = jnp.sin(ang) * sign_ref[...]


def _rope_tables(positions):
    n = positions.size
    tm = min(2048, n)
    inv_freq = ROPE_BASE ** (-jnp.arange(0, ROPE, 2, dtype=F32) / ROPE)
    freq = jnp.tile(inv_freq, LANES // (ROPE // 2))[None, :]
    sign = jnp.tile(jnp.concatenate([-jnp.ones((ROPE // 2,), F32),
                                     jnp.ones((ROPE // 2,), F32)]), LANES // ROPE)[None, :]
    pos = jnp.broadcast_to(positions.reshape(n, 1), (n, LANES))
    row = pl.BlockSpec((tm, LANES), lambda i: (i, 0))
    const = pl.BlockSpec((1, LANES), lambda i: (0, 0))
    return pl.pallas_call(
        _rope_table_kernel,
        grid=(n // tm,),
        in_specs=[row, const, const],
        out_specs=[row, row],
        out_shape=[jax.ShapeDtypeStruct((n, LANES), F32)] * 2,
        compiler_params=_params("parallel"),
        name="rope_tables",
    )(pos, freq, sign)


def _postnorm_kernel(f_ref, h_ref, gp_ref, gn_ref, ho_ref, a_ref):
    h_new = h_ref[...] + _rms(f_ref[...], gp_ref[...])
    ho_ref[...] = h_new
    a_ref[...] = _rms(h_new, gn_ref[...]).astype(BF16)


def _postnorm_last_kernel(f_ref, h_ref, gp_ref, ho_ref):
    ho_ref[...] = h_ref[...] + _rms(f_ref[...], gp_ref[...])


def _postnorm(f, h, g_post, g_next):
    m, d = f.shape
    tm = 256
    row = pl.BlockSpec((tm, d), lambda i: (i, 0))
    vec = pl.BlockSpec((1, d), lambda i: (0, 0))
    if g_next is None:
        return pl.pallas_call(
            _postnorm_last_kernel,
            grid=(m // tm,),
            in_specs=[row, row, vec],
            out_specs=row,
            out_shape=jax.ShapeDtypeStruct((m, d), F32),
            compiler_params=_params("parallel"),
            name="postnorm_last",
        )(f, h, g_post[None, :]), None
    return pl.pallas_call(
        _postnorm_kernel,
        grid=(m // tm,),
        in_specs=[row, row, vec, vec],
        out_specs=[row, row],
        out_shape=[jax.ShapeDtypeStruct((m, d), F32),
                   jax.ShapeDtypeStruct((m, d), BF16)],
        compiler_params=_params("parallel"),
        name="postnorm",
    )(f, h, g_post[None, :], g_next[None, :])


def _mm_kernel(x_ref, w_ref, o_ref):
    o_ref[...] = jnp.dot(x_ref[...], w_ref[...],
                         preferred_element_type=F32).astype(o_ref.dtype)


def _matmul(x, w, layer, *, tm, tn, out_dtype, name="matmul"):
    m, k = x.shape
    tm = min(tm, m)
    if layer is None:
        assert w.shape[2] == tn
        n = w.shape[0] * tn
        w_spec = pl.BlockSpec((None, k, tn), lambda i, j: (j, 0, 0))
    else:
        n = w.shape[2]
        w_spec = _stacked_w_spec(k, tn, layer)
    return pl.pallas_call(
        _mm_kernel,
        grid=(m // tm, n // tn),
        in_specs=[pl.BlockSpec((tm, k), lambda i, j: (i, 0)), w_spec],
        out_specs=pl.BlockSpec((tm, tn), lambda i, j: (i, j)),
        out_shape=jax.ShapeDtypeStruct((m, n), out_dtype),
        compiler_params=_params("parallel", "parallel"),
        name=name,
    )(x, w)


def _outproj_postnorm_kernel(x_ref, w_ref, h_ref, gp_ref, gn_ref, ho_ref, a_ref,
                             f_even, f_odd, *, n_row_blocks):
    i = pl.program_id(0)
    j = pl.program_id(1)
    nj, _, _ = f_even.shape
    slab = ho_ref.shape[0]

    def matmul_tile(f_scr):
        f_scr[j] = jnp.dot(x_ref[...], w_ref[...], preferred_element_type=F32)

    def epilogue(f_scr):
        rows = pl.ds(pl.multiple_of(j * slab, slab), slab)
        f = jnp.concatenate([f_scr[c, rows, :] for c in range(nj)], axis=1)
        h_new = h_ref[...] + _rms(f, gp_ref[...])
        ho_ref[...] = h_new
        a_ref[...] = _rms(h_new, gn_ref[...]).astype(BF16)

    @pl.when(i == 0)
    def _():
        matmul_tile(f_even)
        ho_ref[...] = jnp.zeros(ho_ref.shape, F32)
        a_ref[...] = jnp.zeros(a_ref.shape, BF16)

    steady = jnp.logical_and(i > 0, i < n_row_blocks)

    @pl.when(jnp.logical_and(steady, i % 2 == 0))
    def _():
        matmul_tile(f_even)
        epilogue(f_odd)

    @pl.when(jnp.logical_and(steady, i % 2 == 1))
    def _():
        matmul_tile(f_odd)
        epilogue(f_even)

    @pl.when(i == n_row_blocks)
    def _():
        epilogue(f_odd if n_row_blocks % 2 == 0 else f_even)


def _outproj_postnorm(x, w, layer, h, g_post, g_next, *, tm, tn, name):
    m, k = x.shape
    n = w.shape[2]
    tm = min(tm, m)
    nb = m // tm
    nj = n // tn
    slab = tm // nj
    last = nb - 1

    def in_slab(i, j):
        return (jnp.clip(i - 1, 0, last) * nj + j, 0)

    def out_slab(i, j):
        return (jnp.where(i == 0, nb, i - 1) * nj + j, 0)

    vec = pl.BlockSpec((1, n), lambda i, j: (0, 0))
    out_spec = pl.BlockSpec((slab, n), out_slab)
    return pl.pallas_call(
        functools.partial(_outproj_postnorm_kernel, n_row_blocks=nb),
        grid=(nb + 1, nj),
        in_specs=[pl.BlockSpec((tm, k), lambda i, j: (jnp.minimum(i, last), 0)),
                  _stacked_w_spec(k, tn, layer),
                  pl.BlockSpec((slab, n), in_slab), vec, vec],
        out_specs=[out_spec, out_spec],
        out_shape=[jax.ShapeDtypeStruct((m + tm, n), F32),
                   jax.ShapeDtypeStruct((m + tm, n), BF16)],
        scratch_shapes=[pltpu.VMEM((nj, tm, tn), F32),
                        pltpu.VMEM((nj, tm, tn), F32)],
        compiler_params=_params("arbitrary", "arbitrary"),
        name=name,
    )(x, w, h, g_post[None, :], g_next[None, :])


def _swiglu_kernel(x_ref, wg_ref, wu_ref, wd_ref, o_ref, wd_bf16_ref):
    x = x_ref[...]
    g = jnp.dot(x, wg_ref[...].astype(BF16), preferred_element_type=F32)
    u = jnp.dot(x, wu_ref[...].astype(BF16), preferred_element_type=F32)
    o_ref[...] = (g * jax.nn.sigmoid(g) * u).astype(o_ref.dtype)
    n_tiles, _, tile = wd_bf16_ref.shape
    for c in range(n_tiles):
        wd_bf16_ref[c] = wd_ref[:, c * tile:(c + 1) * tile].astype(BF16)


def _swiglu_up(x, w_gate_up, w_down, layer, *, rows, tm, down_tn):
    m, k = rows, x.shape[1]
    tm = min(tm, m)
    tn = MXU_WIDTH
    nj = FFN_HIDDEN // tn
    n_steps = (m // tm) * nj
    assert FFN_HIDDEN % n_steps == 0
    slab = FFN_HIDDEN // n_steps
    d = w_down.shape[2]
    n_tiles = d // down_tn
    return pl.pallas_call(
        _swiglu_kernel,
        grid=(m // tm, nj),
        in_specs=[pl.BlockSpec((tm, k), lambda i, j: (i, 0)),
                  _stacked_w_spec(k, tn, layer),
                  _stacked_w_spec(k, tn, layer, nj),
                  pl.BlockSpec((None, slab, d), lambda i, j: (layer, i * nj + j, 0))],
        out_specs=[pl.BlockSpec((tm, tn), lambda i, j: (i, j)),
                   pl.BlockSpec((n_tiles, slab, down_tn), lambda i, j: (0, i * nj + j, 0))],
        out_shape=[jax.ShapeDtypeStruct((m, FFN_HIDDEN), BF16),
                   jax.ShapeDtypeStruct((n_tiles, FFN_HIDDEN, down_tn), BF16)],
        compiler_params=_params("parallel", "parallel"),
        name="swiglu_up",
    )(x, w_gate_up, w_gate_up, w_down)


def _gmlp_in_kernel(*refs, n_row_blocks, has_residual):
    if has_residual:
        (f_ref, h_ref, gp_ref, gn_ref, wu_ref, wv_ref,
         ho_ref, u_ref, v_ref, a_even, a_odd) = refs
    else:
        h_ref, gn_ref, wu_ref, wv_ref, u_ref, v_ref, a_even, a_odd = refs
    i = pl.program_id(0)
    j = pl.program_id(1)
    slab = h_ref.shape[0]

    def norm_slab(a_scr):
        rows = pl.ds(pl.multiple_of(j * slab, slab), slab)
        h = h_ref[...]
        if has_residual:
            h = h + _rms(f_ref[...], gp_ref[...])
            ho_ref[...] = h
        a_scr[rows, :] = _rms(h, gn_ref[...]).astype(BF16)

    def matmul_tile(a_scr):
        a = a_scr[...]
        u_ref[...] = _gelu_exact(
            jnp.dot(a, wu_ref[...], preferred_element_type=F32)).astype(BF16)
        v_ref[...] = _gelu_exact(jnp.dot(a, wv_ref[...], preferred_element_type=F32))

    @pl.when(i == 0)
    def _():
        norm_slab(a_even)
        u_ref[...] = jnp.zeros(u_ref.shape, BF16)
        v_ref[...] = jnp.zeros(v_ref.shape, F32)

    steady = jnp.logical_and(i > 0, i < n_row_blocks)

    @pl.when(jnp.logical_and(steady, i % 2 == 0))
    def _():
        norm_slab(a_even)
        matmul_tile(a_odd)

    @pl.when(jnp.logical_and(steady, i % 2 == 1))
    def _():
        norm_slab(a_odd)
        matmul_tile(a_even)

    @pl.when(i == n_row_blocks)
    def _():
        matmul_tile(a_odd if n_row_blocks % 2 == 0 else a_even)
        if has_residual:
            ho_ref[...] = jnp.zeros(ho_ref.shape, F32)


def _gmlp_in(h, f, g_post, g_pre, w_in, layer, *, rows, tm, tn):
    m, k = rows, h.shape[1]
    tm = min(tm, m)
    nb = m // tm
    nj = D_MODEL // tn
    slab = tm // nj
    last = nb - 1
    has_residual = f is not None

    slab_in = pl.BlockSpec((slab, k), lambda i, j: (jnp.minimum(i, last) * nj + j, 0))
    slab_out = pl.BlockSpec((slab, k), lambda i, j: (i * nj + j, 0))
    vec = pl.BlockSpec((1, k), lambda i, j: (0, 0))
    tile_out = pl.BlockSpec((tm, tn), lambda i, j: (jnp.where(i == 0, nb, i - 1), j))
    w_specs = [_stacked_w_spec(k, tn, layer), _stacked_w_spec(k, tn, layer, nj)]
    padded = m + tm
    uv_shapes = [jax.ShapeDtypeStruct((padded, D_MODEL), BF16),
                 jax.ShapeDtypeStruct((padded, D_MODEL), F32)]
    if has_residual:
        in_specs = [slab_in, slab_in, vec, vec] + w_specs
        out_specs = [slab_out, tile_out, tile_out]
        out_shape = [jax.ShapeDtypeStruct((padded, k), F32)] + uv_shapes
        args = (f, h, g_post[None, :], g_pre[None, :], w_in, w_in)
    else:
        in_specs = [slab_in, vec] + w_specs
        out_specs = [tile_out, tile_out]
        out_shape = uv_shapes
        args = (h, g_pre[None, :], w_in, w_in)
    return pl.pallas_call(
        functools.partial(_gmlp_in_kernel, n_row_blocks=nb, has_residual=has_residual),
        grid=(nb + 1, nj),
        in_specs=in_specs,
        out_specs=out_specs,
        out_shape=out_shape,
        scratch_shapes=[pltpu.VMEM((tm, k), BF16), pltpu.VMEM((tm, k), BF16)],
        compiler_params=_params("arbitrary", "arbitrary"),
        name="gmlp_in",
    )(*args)


GMLP_ROWS = 256


def _gmlp_spatial_kernel(u_ref, v_ref, lg_ref, lb_ref, ws_ref, bs_ref, y_ref):
    v = v_ref[...]
    mu = jnp.mean(v, axis=-1, keepdims=True)
    vc = v - mu
    vn = vc * lax.rsqrt(jnp.mean(vc * vc, axis=-1, keepdims=True) + LN_EPS)
    vn = (vn * lg_ref[...] + lb_ref[...]).astype(BF16)
    t_idx = lax.broadcasted_iota(jnp.int32, (CHUNK, CHUNK), 0)
    s_idx = lax.broadcasted_iota(jnp.int32, (CHUNK, CHUNK), 1)
    causal = s_idx <= t_idx
    for g in range(GMLP_GROUPS):
        w = jnp.where(causal, ws_ref[g], 0.0).astype(BF16)
        bias = bs_ref[:, g:g + 1]
        cols = slice(g * GROUP_DIM, (g + 1) * GROUP_DIM)
        for c in range(GMLP_ROWS // CHUNK):
            rows = slice(c * CHUNK, (c + 1) * CHUNK)
            mixed = jnp.dot(w, vn[rows, cols], preferred_element_type=F32) + bias
            y_ref[rows, cols] = (u_ref[rows, cols].astype(F32) * mixed).astype(BF16)


def _gmlp_spatial(u, v, ln_g, ln_b, w_s, b_s, *, rows):
    m, d = rows, u.shape[1]
    tm = GMLP_ROWS
    row = pl.BlockSpec((tm, d), lambda i: (i, 0))
    vec = pl.BlockSpec((1, d), lambda i: (0, 0))
    return pl.pallas_call(
        _gmlp_spatial_kernel,
        grid=(m // tm,),
        in_specs=[row, row, vec, vec,
                  pl.BlockSpec((GMLP_GROUPS, CHUNK, CHUNK), lambda i: (0, 0, 0)),
                  pl.BlockSpec((CHUNK, GMLP_GROUPS), lambda i: (0, 0))],
        out_specs=row,
        out_shape=jax.ShapeDtypeStruct((m, d), BF16),
        compiler_params=_params("parallel"),
        name="gmlp_spatial",
    )(u, v, ln_g[None, :], ln_b[None, :], w_s, b_s.T)


def _mla_down_kernel(x_ref, w_ref, gq_ref, gkv_ref, cos_ref, sin_ref,
                     cq_ref, ckv_ref, kr_ref):
    c = jnp.dot(x_ref[...], w_ref[...], preferred_element_type=F32)
    cq_ref[...] = _rms(c[:, :Q_RANK], gq_ref[...]).astype(BF16)
    ckv_ref[...] = _rms(c[:, Q_RANK:Q_RANK + KV_RANK], gkv_ref[...]).astype(BF16)
    kr = c[:, Q_RANK + KV_RANK:]
    half = ROPE // 2
    swapped = jnp.concatenate([kr[:, half:], kr[:, :half]], axis=-1)
    kr_ref[...] = (kr * cos_ref[:, :ROPE] + swapped * sin_ref[:, :ROPE]).astype(BF16)


def _mla_down(a, w_dqkv, layer, gq, gkv, cos_t, sin_t):
    m, k = a.shape
    n = w_dqkv.shape[2]
    tm = 512
    tab = pl.BlockSpec((tm, LANES), lambda i: (i, 0))
    return pl.pallas_call(
        _mla_down_kernel,
        grid=(m // tm,),
        in_specs=[pl.BlockSpec((tm, k), lambda i: (i, 0)),
                  pl.BlockSpec((None, k, n), lambda i: (layer, 0, 0)),
                  pl.BlockSpec((1, Q_RANK), lambda i: (0, 0)),
                  pl.BlockSpec((1, KV_RANK), lambda i: (0, 0)),
                  tab, tab],
        out_specs=[pl.BlockSpec((tm, Q_RANK), lambda i: (i, 0)),
                   pl.BlockSpec((tm, KV_RANK), lambda i: (i, 0)),
                   pl.BlockSpec((tm, ROPE), lambda i: (i, 0))],
        out_shape=[jax.ShapeDtypeStruct((m, Q_RANK), BF16),
                   jax.ShapeDtypeStruct((m, KV_RANK), BF16),
                   jax.ShapeDtypeStruct((m, ROPE), BF16)],
        compiler_params=_params("parallel"),
        name="mla_down",
    )(a, w_dqkv, gq[None, :], gkv[None, :], cos_t, sin_t)


def _q_rope_kernel(x_ref, w_ref, cos_ref, sin_ref, o_ref):
    r = jnp.dot(x_ref[...], w_ref[...], preferred_element_type=F32)
    cos = cos_ref[...]
    sin = sin_ref[...]
    for t in range(r.shape[1] // LANES):
        x = r[:, t * LANES:(t + 1) * LANES]
        o_ref[:, t * LANES:(t + 1) * LANES] = (
            x * cos + _swap_halves_32(x) * sin).astype(BF16)


def _q_rope(cq, w_rope, layer, cos_t, sin_t):
    m, k = cq.shape
    n = w_rope.shape[2]
    tm, tn = min(2048, m), 1024
    tab = pl.BlockSpec((tm, LANES), lambda i, j: (i, 0))
    return pl.pallas_call(
        _q_rope_kernel,
        grid=(m // tm, n // tn),
        in_specs=[pl.BlockSpec((tm, k), lambda i, j: (i, 0)),
                  _stacked_w_spec(k, tn, layer),
                  tab, tab],
        out_specs=pl.BlockSpec((tm, tn), lambda i, j: (i, j)),
        out_shape=jax.ShapeDtypeStruct((m, n), BF16),
        compiler_params=_params("parallel", "parallel"),
        name="q_rope",
    )(cq, w_rope, cos_t, sin_t)


ATT_TQ = 512
ATT_TK = 512
HEADS_PER_STEP = 4
QK_DIM = NOPE + ROPE
QK_PAD = MXU_WIDTH
EXP2_SCALE = np.float32(QK_DIM ** -0.5 * math.log2(math.e))


def _attention_kernel(qn_ref, qr_ref, kn_ref, kr_ref, v_ref, o_ref,
                      q_scr, k_scr, m_scr, l_scr, acc_scr):
    qi = pl.program_id(2)
    seq = kn_ref.shape[0]

    @pl.when(qi == 0)
    def _():
        for h in range(HEADS_PER_STEP):
            k_scr[h, :, :NOPE] = kn_ref[:, h * NOPE:(h + 1) * NOPE]
            k_scr[h, :, NOPE:QK_DIM] = kr_ref[...]
            k_scr[h, :, QK_DIM:] = jnp.zeros((seq, QK_PAD - QK_DIM), BF16)
            q_scr[h, :, QK_DIM:] = jnp.zeros((ATT_TQ, QK_PAD - QK_DIM), BF16)

    for h in range(HEADS_PER_STEP):
        q_scr[h, :, :NOPE] = qn_ref[:, h * NOPE:(h + 1) * NOPE]
        q_scr[h, :, NOPE:QK_DIM] = qr_ref[:, h * ROPE:(h + 1) * ROPE]
    def key_block(j, on_diagonal):
        rows = pl.ds(j * ATT_TK, ATT_TK)
        first = j == 0
        for h in range(HEADS_PER_STEP):
            s = lax.dot_general(q_scr[h], k_scr[h, rows, :], (((1,), (1,)), ((), ())),
                                preferred_element_type=F32)
            if on_diagonal:
                r_idx = lax.broadcasted_iota(jnp.int32, s.shape, 0)
                c_idx = lax.broadcasted_iota(jnp.int32, s.shape, 1)
                s = jnp.where(c_idx <= r_idx, s, -jnp.inf)
            m_new = jnp.broadcast_to(jnp.max(s, axis=1, keepdims=True), (ATT_TQ, LANES))
            if not first:
                m_prev = m_scr[h]
                m_new = jnp.maximum(m_prev, m_new)
            m_wide = jnp.concatenate([m_new] * (ATT_TK // LANES), axis=1)
            p = jnp.exp2((s - m_wide) * EXP2_SCALE)
            l_new = jnp.broadcast_to(jnp.sum(p, axis=1, keepdims=True), (ATT_TQ, LANES))
            acc_new = jnp.dot(p.astype(BF16), v_ref[rows, h * V_DIM:(h + 1) * V_DIM],
                              preferred_element_type=F32)
            if not first:
                alpha = jnp.exp2((m_prev - m_new) * EXP2_SCALE)
                l_new = alpha * l_scr[h] + l_new
                acc_new = alpha * acc_scr[h] + acc_new
            l_scr[h] = l_new
            acc_scr[h] = acc_new
            m_scr[h] = m_new

    for n_below in range(seq // ATT_TQ):
        @pl.when(qi == n_below)
        def _(n_below=n_below):
            for j in range(n_below):
                key_block(j, False)
            key_block(n_below, True)

    for h in range(HEADS_PER_STEP):
        o_ref[:, h * V_DIM:(h + 1) * V_DIM] = (acc_scr[h] / l_scr[h]).astype(BF16)


def _attention(q_nope, q_rope, kv, k_rope, batch, seq):
    assert ATT_TQ == ATT_TK and V_DIM == LANES
    m = batch * seq
    nq = seq // ATT_TQ
    hp = HEADS // HEADS_PER_STEP
    wn = HEADS_PER_STEP * NOPE
    wr = HEADS_PER_STEP * ROPE
    wv = HEADS_PER_STEP * V_DIM
    v_off = HEADS * NOPE // wv
    return pl.pallas_call(
        _attention_kernel,
        grid=(batch, hp, nq),
        in_specs=[
            pl.BlockSpec((ATT_TQ, wn), lambda b, p, i: (b * nq + i, p)),
            pl.BlockSpec((ATT_TQ, wr), lambda b, p, i: (b * nq + i, p)),
            pl.BlockSpec((seq, wn), lambda b, p, i: (b, p)),
            pl.BlockSpec((seq, ROPE), lambda b, p, i: (b, 0)),
            pl.BlockSpec((seq, wv), lambda b, p, i: (b, v_off + p)),
        ],
        out_specs=pl.BlockSpec((ATT_TQ, wv), lambda b, p, i: (b * nq + i, p)),
        out_shape=jax.ShapeDtypeStruct((m, HEADS * V_DIM), BF16),
        scratch_shapes=[
            pltpu.VMEM((HEADS_PER_STEP, ATT_TQ, QK_PAD), BF16),
            pltpu.VMEM((HEADS_PER_STEP, seq, QK_PAD), BF16),
            pltpu.VMEM((HEADS_PER_STEP, ATT_TQ, LANES), F32),
            pltpu.VMEM((HEADS_PER_STEP, ATT_TQ, LANES), F32),
            pltpu.VMEM((HEADS_PER_STEP, ATT_TQ, V_DIM), F32),
        ],
        compiler_params=_params("parallel", "parallel", "arbitrary"),
        name="mla_attention",
    )(q_nope, q_rope, kv, k_rope, kv)


def _gmlp_layer(h, pending, g_pre, g_post, g_next, j, rows, w_in, ln_g, ln_b, w_s, b_s,
                w_out):
    if pending is None:
        u, v = _gmlp_in(h, None, None, g_pre, w_in, j, rows=rows, tm=1024, tn=512)
    else:
        f, g_prev_post = pending
        h, u, v = _gmlp_in(h, f, g_prev_post, g_pre, w_in, j, rows=rows, tm=1024, tn=512)
    y = _gmlp_spatial(u, v, ln_g[j], ln_b[j], w_s[j], b_s[j], rows=rows)
    return _outproj_postnorm(y, w_out, j, h, g_post, g_next, tm=512, tn=1024,
                             name="gmlp_out")


def _mla_layer(a, h, g_post, g_next, j, cos_t, sin_t, w_dqkv, gq, gkv, w_q_nope, w_q_rope,
               w_kv, w_o, batch, seq):
    cq, ckv, k_rope = _mla_down(a, w_dqkv, j, gq[j], gkv[j], cos_t, sin_t)
    q_nope = _matmul(cq, w_q_nope, j, tm=2048, tn=2048, out_dtype=BF16, name="q_nope")
    q_rope = _q_rope(cq, w_q_rope, j, cos_t, sin_t)
    kv = _matmul(ckv, w_kv, j, tm=2048, tn=2048, out_dtype=BF16, name="kv_up")
    o = _attention(q_nope, q_rope, kv, k_rope, batch, seq)
    return _outproj_postnorm(o, w_o, j, h, g_post, g_next, tm=512, tn=1024,
                             name="attn_out")


def _ffn(a, rows, i, w_gate_up, w_down):
    down_tn = 512
    hid, w_down_tiles = _swiglu_up(a, w_gate_up, w_down, i, rows=rows, tm=2048,
                                   down_tn=down_tn)
    return _matmul(hid, w_down_tiles, None, tm=512, tn=down_tn, out_dtype=F32,
                   name="ffn_down")


def kernel(x, positions, norm_g, gmlp_w_in, gmlp_ln_g, gmlp_ln_b, gmlp_w_s, gmlp_b_s,
           gmlp_w_out, mla_w_dqkv, mla_q_norm_g, mla_kv_norm_g, mla_w_uq, mla_w_ukv,
           mla_w_o, ffn_w_gate_up, ffn_w_down):
    batch, seq, d = x.shape
    n_mla = mla_w_uq.shape[0]
    gmlp_w_in = gmlp_w_in.astype(BF16)
    gmlp_w_out = gmlp_w_out.astype(BF16)
    mla_w_dqkv = mla_w_dqkv.astype(BF16)
    mla_w_o = mla_w_o.astype(BF16)
    w_uq = mla_w_uq.astype(BF16).reshape(n_mla, Q_RANK, HEADS, NOPE + ROPE)
    w_q_nope = w_uq[..., :NOPE].reshape(n_mla, Q_RANK, HEADS * NOPE)
    w_q_rope = w_uq[..., NOPE:].reshape(n_mla, Q_RANK, HEADS * ROPE)
    w_ukv = mla_w_ukv.astype(BF16).reshape(n_mla, KV_RANK, HEADS, NOPE + V_DIM)
    w_kv = jnp.concatenate([w_ukv[..., :NOPE].reshape(n_mla, KV_RANK, HEADS * NOPE),
                            w_ukv[..., NOPE:].reshape(n_mla, KV_RANK, HEADS * V_DIM)], axis=2)

    cos_t, sin_t = _rope_tables(positions)
    rows = batch * seq
    h = x.reshape(rows, d)
    a = None
    pending = None
    for i in range(DEPTH):
        j = i // 2
        if i % 2 == 0:
            h, a = _gmlp_layer(h, pending, norm_g[i, 0], norm_g[i, 1], norm_g[i, 2], j, rows,
                               gmlp_w_in, gmlp_ln_g, gmlp_ln_b, gmlp_w_s, gmlp_b_s,
                               gmlp_w_out)
            pending = None
        else:
            h, a = _mla_layer(a, h, norm_g[i, 1], norm_g[i, 2], j, cos_t, sin_t, mla_w_dqkv,
                              mla_q_norm_g, mla_kv_norm_g, w_q_nope, w_q_rope, w_kv, mla_w_o,
                              batch, seq)
        f = _ffn(a, rows, i, ffn_w_gate_up, ffn_w_down)
        if i + 1 < DEPTH and (i + 1) % 2 == 0:
            pending, a = (f, norm_g[i, 3]), None
        else:
            g_next = norm_g[i + 1, 0] if i + 1 < DEPTH else None
            h, a = _postnorm(f, h, norm_g[i, 3], g_next)
    return h.reshape(batch, seq, d)
```

```python
import functools
import math

import numpy as np
import jax
import jax.numpy as jnp
from jax import lax
from jax.experimental import pallas as pl
from jax.experimental.pallas import tpu as pltpu

D_MODEL = 4096
DEPTH = 4
CHUNK = 128
GMLP_GROUPS = 32
GROUP_DIM = 128
HEADS = 32
Q_RANK = 1024
KV_RANK = 512
NOPE = 128
ROPE = 64
V_DIM = 128
ROPE_BASE = 10000.0
FFN_HIDDEN = 11008
RMS_EPS = 1e-6
LN_EPS = 1e-5

F32 = jnp.float32
BF16 = jnp.bfloat16

VMEM_LIMIT_BYTES = 60000 * 1024
LANES = 128
MXU_WIDTH = 256


def _params(*semantics):
    return pltpu.CompilerParams(dimension_semantics=semantics,
                                vmem_limit_bytes=VMEM_LIMIT_BYTES)


def _stacked_w_spec(k, tn, layer, col_block_offset=0):
    return pl.BlockSpec((None, k, tn), lambda i, j: (layer, 0, col_block_offset + j))


def _rms(x, g):
    return x * lax.rsqrt(jnp.mean(x * x, axis=-1, keepdims=True) + RMS_EPS) * g


def _gelu_exact(x):
    return 0.5 * x * (1.0 + lax.erf(x * np.float32(math.sqrt(0.5))))


def _swap_halves_32(x):
    lane = lax.broadcasted_iota(jnp.int32, x.shape, 1)
    first_half = (lane % ROPE) < (ROPE // 2)
    return jnp.where(first_half, pltpu.roll(x, LANES - ROPE // 2, 1),
                     pltpu.roll(x, ROPE // 2, 1))


def _rope_table_kernel(pos_ref, freq_ref, sign_ref, cos_ref, sin_ref):
    ang = pos_ref[...].astype(F32) * freq_ref[...]
    cos_ref[...] = jnp.cos(ang)
    sin_ref[...] = jnp.sin(ang) * sign_ref[...]


def _rope_tables(positions):
    n = positions.size
    tm = min(2048, n)
    inv_freq = ROPE_BASE ** (-jnp.arange(0, ROPE, 2, dtype=F32) / ROPE)
    freq = jnp.tile(inv_freq, LANES // (ROPE // 2))[None, :]
    sign = jnp.tile(jnp.concatenate([-jnp.ones((ROPE // 2,), F32),
                                     jnp.ones((ROPE // 2,), F32)]), LANES // ROPE)[None, :]
    pos = jnp.broadcast_to(positions.reshape(n, 1), (n, LANES))
    row = pl.BlockSpec((tm, LANES), lambda i: (i, 0))
    const = pl.BlockSpec((1, LANES), lambda i: (0, 0))
    return pl.pallas_call(
        _rope_table_kernel,
        grid=(n // tm,),
        in_specs=[row, const, const],
        out_specs=[row, row],
        out_shape=[jax.ShapeDtypeStruct((n, LANES), F32)] * 2,
        compiler_params=_params("parallel"),
        name="rope_tables",
    )(pos, freq, sign)


def _postnorm_kernel(f_ref, h_ref, gp_ref, gn_ref, ho_ref, a_ref):
    h_new = h_ref[...] + _rms(f_ref[...], gp_ref[...])
    ho_ref[...] = h_new
    a_ref[...] = _rms(h_new, gn_ref[...]).astype(BF16)


def _postnorm_last_kernel(f_ref, h_ref, gp_ref, ho_ref):
    ho_ref[...] = h_ref[...] + _rms(f_ref[...], gp_ref[...])


def _postnorm(f, h, g_post, g_next):
    m, d = f.shape
    tm = 256
    row = pl.BlockSpec((tm, d), lambda i: (i, 0))
    vec = pl.BlockSpec((1, d), lambda i: (0, 0))
    if g_next is None:
        return pl.pallas_call(
            _postnorm_last_kernel,
            grid=(m // tm,),
            in_specs=[row, row, vec],
            out_specs=row,
            out_shape=jax.ShapeDtypeStruct((m, d), F32),
            compiler_params=_params("parallel"),
            name="postnorm_last",
        )(f, h, g_post[None, :]), None
    return pl.pallas_call(
        _postnorm_kernel,
        grid=(m // tm,),
        in_specs=[row, row, vec, vec],
        out_specs=[row, row],
        out_shape=[jax.ShapeDtypeStruct((m, d), F32),
                   jax.ShapeDtypeStruct((m, d), BF16)],
        compiler_params=_params("parallel"),
        name="postnorm",
    )(f, h, g_post[None, :], g_next[None, :])


def _mm_kernel(x_ref, w_ref, o_ref):
    o_ref[...] = jnp.dot(x_ref[...], w_ref[...],
                         preferred_element_type=F32).astype(o_ref.dtype)


def _matmul(x, w, layer, *, tm, tn, out_dtype, name="matmul"):
    m, k = x.shape
    tm = min(tm, m)
    if layer is None:
        assert w.shape[2] == tn
        n = w.shape[0] * tn
        w_spec = pl.BlockSpec((None, k, tn), lambda i, j: (j, 0, 0))
    else:
        n = w.shape[2]
        w_spec = _stacked_w_spec(k, tn, layer)
    return pl.pallas_call(
        _mm_kernel,
        grid=(m // tm, n // tn),
        in_specs=[pl.BlockSpec((tm, k), lambda i, j: (i, 0)), w_spec],
        out_specs=pl.BlockSpec((tm, tn), lambda i, j: (i, j)),
        out_shape=jax.ShapeDtypeStruct((m, n), out_dtype),
        compiler_params=_params("parallel", "parallel"),
        name=name,
    )(x, w)


def _outproj_postnorm_kernel(x_ref, w_ref, h_ref, gp_ref, gn_ref, ho_ref, a_ref,
                             f_even, f_odd, *, n_row_blocks):
    i = pl.program_id(0)
    j = pl.program_id(1)
    nj, _, _ = f_even.shape
    slab = ho_ref.shape[0]

    def matmul_tile(f_scr):
        f_scr[j] = jnp.dot(x_ref[...], w_ref[...], preferred_element_type=F32)

    def epilogue(f_scr):
        rows = pl.ds(pl.multiple_of(j * slab, slab), slab)
        f = jnp.concatenate([f_scr[c, rows, :] for c in range(nj)], axis=1)
        h_new = h_ref[...] + _rms(f, gp_ref[...])
        ho_ref[...] = h_new
        a_ref[...] = _rms(h_new, gn_ref[...]).astype(BF16)

    @pl.when(i == 0)
    def _():
        matmul_tile(f_even)
        ho_ref[...] = jnp.zeros(ho_ref.shape, F32)
        a_ref[...] = jnp.zeros(a_ref.shape, BF16)

    steady = jnp.logical_and(i > 0, i < n_row_blocks)

    @pl.when(jnp.logical_and(steady, i % 2 == 0))
    def _():
        matmul_tile(f_even)
        epilogue(f_odd)

    @pl.when(jnp.logical_and(steady, i % 2 == 1))
    def _():
        matmul_tile(f_odd)
        epilogue(f_even)

    @pl.when(i == n_row_blocks)
    def _():
        epilogue(f_odd if n_row_blocks % 2 == 0 else f_even)


def _outproj_postnorm(x, w, layer, h, g_post, g_next, *, tm, tn, name):
    m, k = x.shape
    n = w.shape[2]
    tm = min(tm, m)
    nb = m // tm
    nj = n // tn
    slab = tm // nj
    last = nb - 1

    def in_slab(i, j):
        return (jnp.clip(i - 1, 0, last) * nj + j, 0)

    def out_slab(i, j):
        return (jnp.where(i == 0, nb, i - 1) * nj + j, 0)

    vec = pl.BlockSpec((1, n), lambda i, j: (0, 0))
    out_spec = pl.BlockSpec((slab, n), out_slab)
    return pl.pallas_call(
        functools.partial(_outproj_postnorm_kernel, n_row_blocks=nb),
        grid=(nb + 1, nj),
        in_specs=[pl.BlockSpec((tm, k), lambda i, j: (jnp.minimum(i, last), 0)),
                  _stacked_w_spec(k, tn, layer),
                  pl.BlockSpec((slab, n), in_slab), vec, vec],
        out_specs=[out_spec, out_spec],
        out_shape=[jax.ShapeDtypeStruct((m + tm, n), F32),
                   jax.ShapeDtypeStruct((m + tm, n), BF16)],
        scratch_shapes=[pltpu.VMEM((nj, tm, tn), F32),
                        pltpu.VMEM((nj, tm, tn), F32)],
        compiler_params=_params("arbitrary", "arbitrary"),
        name=name,
    )(x, w, h, g_post[None, :], g_next[None, :])


def _swiglu_kernel(x_ref, wg_ref, wu_ref, wd_ref, o_ref, wd_bf16_ref):
    x = x_ref[...]
    g = jnp.dot(x, wg_ref[...].astype(BF16), preferred_element_type=F32)
    u = jnp.dot(x, wu_ref[...].astype(BF16), preferred_element_type=F32)
    o_ref[...] = (g * jax.nn.sigmoid(g) * u).astype(o_ref.dtype)
    n_tiles, _, tile = wd_bf16_ref.shape
    for c in range(n_tiles):
        wd_bf16_ref[c] = wd_ref[:, c * tile:(c + 1) * tile].astype(BF16)


def _swiglu_up(x, w_gate_up, w_down, layer, *, rows, tm, down_tn):
    m, k = rows, x.shape[1]
    tm = min(tm, m)
    tn = MXU_WIDTH
    nj = FFN_HIDDEN // tn
    n_steps = (m // tm) * nj
    assert FFN_HIDDEN % n_steps == 0
    slab = FFN_HIDDEN // n_steps
    d = w_down.shape[2]
    n_tiles = d // down_tn
    return pl.pallas_call(
        _swiglu_kernel,
        grid=(m // tm, nj),
        in_specs=[pl.BlockSpec((tm, k), lambda i, j: (i, 0)),
                  _stacked_w_spec(k, tn, layer),
                  _stacked_w_spec(k, tn, layer, nj),
                  pl.BlockSpec((None, slab, d), lambda i, j: (layer, i * nj + j, 0))],
        out_specs=[pl.BlockSpec((tm, tn), lambda i, j: (i, j)),
                   pl.BlockSpec((n_tiles, slab, down_tn), lambda i, j: (0, i * nj + j, 0))],
        out_shape=[jax.ShapeDtypeStruct((m, FFN_HIDDEN), BF16),
                   jax.ShapeDtypeStruct((n_tiles, FFN_HIDDEN, down_tn), BF16)],
        compiler_params=_params("parallel", "parallel"),
        name="swiglu_up",
    )(x, w_gate_up, w_gate_up, w_down)


def _gmlp_in_kernel(*refs, n_row_blocks, has_residual):
    if has_residual:
        (f_ref, h_ref, gp_ref, gn_ref, wu_ref, wv_ref,
         ho_ref, u_ref, v_ref, a_even, a_odd) = refs
    else:
        h_ref, gn_ref, wu_ref, wv_ref, u_ref, v_ref, a_even, a_odd = refs
    i = pl.program_id(0)
    j = pl.program_id(1)
    slab = h_ref.shape[0]

    def norm_slab(a_scr):
        rows = pl.ds(pl.multiple_of(j * slab, slab), slab)
        h = h_ref[...]
        if has_residual:
            h = h + _rms(f_ref[...], gp_ref[...])
            ho_ref[...] = h
        a_scr[rows, :] = _rms(h, gn_ref[...]).astype(BF16)

    def matmul_tile(a_scr):
        a = a_scr[...]
        u_ref[...] = _gelu_exact(
            jnp.dot(a, wu_ref[...], preferred_element_type=F32)).astype(BF16)
        v_ref[...] = _gelu_exact(jnp.dot(a, wv_ref[...], preferred_element_type=F32))

    @pl.when(i == 0)
    def _():
        norm_slab(a_even)
        u_ref[...] = jnp.zeros(u_ref.shape, BF16)
        v_ref[...] = jnp.zeros(v_ref.shape, F32)

    steady = jnp.logical_and(i > 0, i < n_row_blocks)

    @pl.when(jnp.logical_and(steady, i % 2 == 0))
    def _():
        norm_slab(a_even)
        matmul_tile(a_odd)

    @pl.when(jnp.logical_and(steady, i % 2 == 1))
    def _():
        norm_slab(a_odd)
        matmul_tile(a_even)

    @pl.when(i == n_row_blocks)
    def _():
        matmul_tile(a_odd if n_row_blocks % 2 == 0 else a_even)
        if has_residual:
            ho_ref[...] = jnp.zeros(ho_ref.shape, F32)


def _gmlp_in(h, f, g_post, g_pre, w_in, layer, *, rows, tm, tn):
    m, k = rows, h.shape[1]
    tm = min(tm, m)
    nb = m // tm
    nj = D_MODEL // tn
    slab = tm // nj
    last = nb - 1
    has_residual = f is not None

    slab_in = pl.BlockSpec((slab, k), lambda i, j: (jnp.minimum(i, last) * nj + j, 0))
    slab_out = pl.BlockSpec((slab, k), lambda i, j: (i * nj + j, 0))
    vec = pl.BlockSpec((1, k), lambda i, j: (0, 0))
    tile_out = pl.BlockSpec((tm, tn), lambda i, j: (jnp.where(i == 0, nb, i - 1), j))
    w_specs = [_stacked_w_spec(k, tn, layer), _stacked_w_spec(k, tn, layer, nj)]
    padded = m + tm
    uv_shapes = [jax.ShapeDtypeStruct((padded, D_MODEL), BF16),
                 jax.ShapeDtypeStruct((padded, D_MODEL), F32)]
    if has_residual:
        in_specs = [slab_in, slab_in, vec, vec] + w_specs
        out_specs = [slab_out, tile_out, tile_out]
        out_shape = [jax.ShapeDtypeStruct((padded, k), F32)] + uv_shapes
        args = (f, h, g_post[None, :], g_pre[None, :], w_in, w_in)
    else:
        in_specs = [slab_in, vec] + w_specs
        out_specs = [tile_out, tile_out]
        out_shape = uv_shapes
        args = (h, g_pre[None, :], w_in, w_in)
    return pl.pallas_call(
        functools.partial(_gmlp_in_kernel, n_row_blocks=nb, has_residual=has_residual),
        grid=(nb + 1, nj),
        in_specs=in_specs,
        out_specs=out_specs,
        out_shape=out_shape,
        scratch_shapes=[pltpu.VMEM((tm, k), BF16), pltpu.VMEM((tm, k), BF16)],
        compiler_params=_params("arbitrary", "arbitrary"),
        name="gmlp_in",
    )(*args)


GMLP_ROWS = 256


def _gmlp_spatial_kernel(u_ref, v_ref, lg_ref, lb_ref, ws_ref, bs_ref, y_ref):
    v = v_ref[...]
    mu = jnp.mean(v, axis=-1, keepdims=True)
    vc = v - mu
    vn = vc * lax.rsqrt(jnp.mean(vc * vc, axis=-1, keepdims=True) + LN_EPS)
    vn = (vn * lg_ref[...] + lb_ref[...]).astype(BF16)
    t_idx = lax.broadcasted_iota(jnp.int32, (CHUNK, CHUNK), 0)
    s_idx = lax.broadcasted_iota(jnp.int32, (CHUNK, CHUNK), 1)
    causal = s_idx <= t_idx
    for g in range(GMLP_GROUPS):
        w = jnp.where(causal, ws_ref[g], 0.0).astype(BF16)
        bias = bs_ref[:, g:g + 1]
        cols = slice(g * GROUP_DIM, (g + 1) * GROUP_DIM)
        for c in range(GMLP_ROWS // CHUNK):
            rows = slice(c * CHUNK, (c + 1) * CHUNK)
            mixed = jnp.dot(w, vn[rows, cols], preferred_element_type=F32) + bias
            y_ref[rows, cols] = (u_ref[rows, cols].astype(F32) * mixed).astype(BF16)


def _gmlp_spatial(u, v, ln_g, ln_b, w_s, b_s, *, rows):
    m, d = rows, u.shape[1]
    tm = GMLP_ROWS
    row = pl.BlockSpec((tm, d), lambda i: (i, 0))
    vec = pl.BlockSpec((1, d), lambda i: (0, 0))
    return pl.pallas_call(
        _gmlp_spatial_kernel,
        grid=(m // tm,),
        in_specs=[row, row, vec, vec,
                  pl.BlockSpec((GMLP_GROUPS, CHUNK, CHUNK), lambda i: (0, 0, 0)),
                  pl.BlockSpec((CHUNK, GMLP_GROUPS), lambda i: (0, 0))],
        out_specs=row,
        out_shape=jax.ShapeDtypeStruct((m, d), BF16),
        compiler_params=_params("parallel"),
        name="gmlp_spatial",
    )(u, v, ln_g[None, :], ln_b[None, :], w_s, b_s.T)


def _mla_down_kernel(x_ref, w_ref, gq_ref, gkv_ref, cos_ref, sin_ref,
                     cq_ref, ckv_ref, kr_ref):
    c = jnp.dot(x_ref[...], w_ref[...], preferred_element_type=F32)
    cq_ref[...] = _rms(c[:, :Q_RANK], gq_ref[...]).astype(BF16)
    ckv_ref[...] = _rms(c[:, Q_RANK:Q_RANK + KV_RANK], gkv_ref[...]).astype(BF16)
    kr = c[:, Q_RANK + KV_RANK:]
    half = ROPE // 2
    swapped = jnp.concatenate([kr[:, half:], kr[:, :half]], axis=-1)
    kr_ref[...] = (kr * cos_ref[:, :ROPE] + swapped * sin_ref[:, :ROPE]).astype(BF16)


def _mla_down(a, w_dqkv, layer, gq, gkv, cos_t, sin_t):
    m, k = a.shape
    n = w_dqkv.shape[2]
    tm = 512
    tab = pl.BlockSpec((tm, LANES), lambda i: (i, 0))
    return pl.pallas_call(
        _mla_down_kernel,
        grid=(m // tm,),
        in_specs=[pl.BlockSpec((tm, k), lambda i: (i, 0)),
                  pl.BlockSpec((None, k, n), lambda i: (layer, 0, 0)),
                  pl.BlockSpec((1, Q_RANK), lambda i: (0, 0)),
                  pl.BlockSpec((1, KV_RANK), lambda i: (0, 0)),
                  tab, tab],
        out_specs=[pl.BlockSpec((tm, Q_RANK), lambda i: (i, 0)),
                   pl.BlockSpec((tm, KV_RANK), lambda i: (i, 0)),
                   pl.BlockSpec((tm, ROPE), lambda i: (i, 0))],
        out_shape=[jax.ShapeDtypeStruct((m, Q_RANK), BF16),
                   jax.ShapeDtypeStruct((m, KV_RANK), BF16),
                   jax.ShapeDtypeStruct((m, ROPE), BF16)],
        compiler_params=_params("parallel"),
        name="mla_down",
    )(a, w_dqkv, gq[None, :], gkv[None, :], cos_t, sin_t)


def _q_rope_kernel(x_ref, w_ref, cos_ref, sin_ref, o_ref):
    r = jnp.dot(x_ref[...], w_ref[...], preferred_element_type=F32)
    cos = cos_ref[...]
    sin = sin_ref[...]
    for t in range(r.shape[1] // LANES):
        x = r[:, t * LANES:(t + 1) * LANES]
        o_ref[:, t * LANES:(t + 1) * LANES] = (
            x * cos + _swap_halves_32(x) * sin).astype(BF16)


def _q_rope(cq, w_rope, layer, cos_t, sin_t):
    m, k = cq.shape
    n = w_rope.shape[2]
    tm, tn = min(2048, m), 1024
    tab = pl.BlockSpec((tm, LANES), lambda i, j: (i, 0))
    return pl.pallas_call(
        _q_rope_kernel,
        grid=(m // tm, n // tn),
        in_specs=[pl.BlockSpec((tm, k), lambda i, j: (i, 0)),
                  _stacked_w_spec(k, tn, layer),
                  tab, tab],
        out_specs=pl.BlockSpec((tm, tn), lambda i, j: (i, j)),
        out_shape=jax.ShapeDtypeStruct((m, n), BF16),
        compiler_params=_params("parallel", "parallel"),
        name="q_rope",
    )(cq, w_rope, cos_t, sin_t)


ATT_TQ = 512
ATT_TK = 512
HEADS_PER_STEP = 4
QK_DIM = NOPE + ROPE
QK_PAD = MXU_WIDTH
EXP2_SCALE = np.float32(QK_DIM ** -0.5 * math.log2(math.e))


def _attention_kernel(qn_ref, qr_ref, kn_ref, kr_ref, v_ref, o_ref,
                      q_scr, k_scr, m_scr, l_scr, acc_scr):
    qi = pl.program_id(2)
    seq = kn_ref.shape[0]

    @pl.when(qi == 0)
    def _():
        for h in range(HEADS_PER_STEP):
            k_scr[h, :, :NOPE] = kn_ref[:, h * NOPE:(h + 1) * NOPE]
            k_scr[h, :, NOPE:QK_DIM] = kr_ref[...]
            k_scr[h, :, QK_DIM:] = jnp.zeros((seq, QK_PAD - QK_DIM), BF16)
            q_scr[h, :, QK_DIM:] = jnp.zeros((ATT_TQ, QK_PAD - QK_DIM), BF16)

    for h in range(HEADS_PER_STEP):
        q_scr[h, :, :NOPE] = qn_ref[:, h * NOPE:(h + 1) * NOPE]
        q_scr[h, :, NOPE:QK_DIM] = qr_ref[:, h * ROPE:(h + 1) * ROPE]
    def key_block(j, on_diagonal, n_blocks=1):
        width = n_blocks * ATT_TK
        rows = pl.ds(j * ATT_TK, width)
        first = j == 0
        for h in range(HEADS_PER_STEP):
            s = lax.dot_general(q_scr[h], k_scr[h, rows, :], (((1,), (1,)), ((), ())),
                                preferred_element_type=F32)
            if on_diagonal:
                r_idx = lax.broadcasted_iota(jnp.int32, s.shape, 0)
                c_idx = lax.broadcasted_iota(jnp.int32, s.shape, 1)
                s = jnp.where(c_idx <= r_idx, s, -jnp.inf)
            m_new = jnp.broadcast_to(jnp.max(s, axis=1, keepdims=True), (ATT_TQ, LANES))
            if not first:
                m_prev = m_scr[h]
                m_new = jnp.maximum(m_prev, m_new)
            m_wide = jnp.concatenate([m_new] * (width // LANES), axis=1)
            p = jnp.exp2((s - m_wide) * EXP2_SCALE)
            l_new = jnp.broadcast_to(jnp.sum(p, axis=1, keepdims=True), (ATT_TQ, LANES))
            acc_new = jnp.dot(p.astype(BF16), v_ref[rows, h * V_DIM:(h + 1) * V_DIM],
                              preferred_element_type=F32)
            if not first:
                alpha = jnp.exp2((m_prev - m_new) * EXP2_SCALE)
                l_new = alpha * l_scr[h] + l_new
                acc_new = alpha * acc_scr[h] + acc_new
            l_scr[h] = l_new
            acc_scr[h] = acc_new
            m_scr[h] = m_new

    for n_below in range(seq // ATT_TQ):
        @pl.when(qi == n_below)
        def _(n_below=n_below):
            for j in range(0, n_below - 1, 2):
                key_block(j, False, 2)
            if n_below % 2 == 1:
                key_block(n_below - 1, False)
            key_block(n_below, True)

    for h in range(HEADS_PER_STEP):
        o_ref[:, h * V_DIM:(h + 1) * V_DIM] = (acc_scr[h] / l_scr[h]).astype(BF16)


def _attention(q_nope, q_rope, kv, k_rope, batch, seq):
    assert ATT_TQ == ATT_TK and V_DIM == LANES
    m = batch * seq
    nq = seq // ATT_TQ
    hp = HEADS // HEADS_PER_STEP
    wn = HEADS_PER_STEP * NOPE
    wr = HEADS_PER_STEP * ROPE
    wv = HEADS_PER_STEP * V_DIM
    v_off = HEADS * NOPE // wv
    return pl.pallas_call(
        _attention_kernel,
        grid=(batch, hp, nq),
        in_specs=[
            pl.BlockSpec((ATT_TQ, wn), lambda b, p, i: (b * nq + i, p)),
            pl.BlockSpec((ATT_TQ, wr), lambda b, p, i: (b * nq + i, p)),
            pl.BlockSpec((seq, wn), lambda b, p, i: (b, p)),
            pl.BlockSpec((seq, ROPE), lambda b, p, i: (b, 0)),
            pl.BlockSpec((seq, wv), lambda b, p, i: (b, v_off + p)),
        ],
        out_specs=pl.BlockSpec((ATT_TQ, wv), lambda b, p, i: (b * nq + i, p)),
        out_shape=jax.ShapeDtypeStruct((m, HEADS * V_DIM), BF16),
        scratch_shapes=[
            pltpu.VMEM((HEADS_PER_STEP, ATT_TQ, QK_PAD), BF16),
            pltpu.VMEM((HEADS_PER_STEP, seq, QK_PAD), BF16),
            pltpu.VMEM((HEADS_PER_STEP, ATT_TQ, LANES), F32),
            pltpu.VMEM((HEADS_PER_STEP, ATT_TQ, LANES), F32),
            pltpu.VMEM((HEADS_PER_STEP, ATT_TQ, V_DIM), F32),
        ],
        compiler_params=_params("parallel", "parallel", "arbitrary"),
        name="mla_attention",
    )(q_nope, q_rope, kv, k_rope, kv)


def _gmlp_layer(h, pending, g_pre, g_post, g_next, j, rows, w_in, ln_g, ln_b, w_s, b_s,
                w_out):
    if pending is None:
        u, v = _gmlp_in(h, None, None, g_pre, w_in, j, rows=rows, tm=1024, tn=512)
    else:
        f, g_prev_post = pending
        h, u, v = _gmlp_in(h, f, g_prev_post, g_pre, w_in, j, rows=rows, tm=1024, tn=512)
    y = _gmlp_spatial(u, v, ln_g[j], ln_b[j], w_s[j], b_s[j], rows=rows)
    return _outproj_postnorm(y, w_out, j, h, g_post, g_next, tm=512, tn=1024,
                             name="gmlp_out")


def _mla_layer(a, h, g_post, g_next, j, cos_t, sin_t, w_dqkv, gq, gkv, w_q_nope, w_q_rope,
               w_kv, w_o, batch, seq):
    cq, ckv, k_rope = _mla_down(a, w_dqkv, j, gq[j], gkv[j], cos_t, sin_t)
    q_nope = _matmul(cq, w_q_nope, j, tm=2048, tn=2048, out_dtype=BF16, name="q_nope")
    q_rope = _q_rope(cq, w_q_rope, j, cos_t, sin_t)
    kv = _matmul(ckv, w_kv, j, tm=2048, tn=2048, out_dtype=BF16, name="kv_up")
    o = _attention(q_nope, q_rope, kv, k_rope, batch, seq)
    return _outproj_postnorm(o, w_o, j, h, g_post, g_next, tm=512, tn=1024,
                             name="attn_out")


def _ffn(a, rows, i, w_gate_up, w_down):
    down_tn = 512
    hid, w_down_tiles = _swiglu_up(a, w_gate_up, w_down, i, rows=rows, tm=2048,
                                   down_tn=down_tn)
    return _matmul(hid, w_down_tiles, None, tm=512, tn=down_tn, out_dtype=F32,
                   name="ffn_down")


def kernel(x, positions, norm_g, gmlp_w_in, gmlp_ln_g, gmlp_ln_b, gmlp_w_s, gmlp_b_s,
           gmlp_w_out, mla_w_dqkv, mla_q_norm_g, mla_kv_norm_g, mla_w_uq, mla_w_ukv,
           mla_w_o, ffn_w_gate_up, ffn_w_down):
    batch, seq, d = x.shape
    n_mla = mla_w_uq.shape[0]
    gmlp_w_in = gmlp_w_in.astype(BF16)
    gmlp_w_out = gmlp_w_out.astype(BF16)
    mla_w_dqkv = mla_w_dqkv.astype(BF16)
    mla_w_o = mla_w_o.astype(BF16)
    w_uq = mla_w_uq.astype(BF16).reshape(n_mla, Q_RANK, HEADS, NOPE + ROPE)
    w_q_nope = w_uq[..., :NOPE].reshape(n_mla, Q_RANK, HEADS * NOPE)
    w_q_rope = w_uq[..., NOPE:].reshape(n_mla, Q_RANK, HEADS * ROPE)
    w_ukv = mla_w_ukv.astype(BF16).reshape(n_mla, KV_RANK, HEADS, NOPE + V_DIM)
    w_kv = jnp.concatenate([w_ukv[..., :NOPE].reshape(n_mla, KV_RANK, HEADS * NOPE),
                            w_ukv[..., NOPE:].reshape(n_mla, KV_RANK, HEADS * V_DIM)], axis=2)

    cos_t, sin_t = _rope_tables(positions)
    rows = batch * seq
    h = x.reshape(rows, d)
    a = None
    pending = None
    for i in range(DEPTH):
        j = i // 2
        if i % 2 == 0:
            h, a = _gmlp_layer(h, pending, norm_g[i, 0], norm_g[i, 1], norm_g[i, 2], j, rows,
                               gmlp_w_in, gmlp_ln_g, gmlp_ln_b, gmlp_w_s, gmlp_b_s,
                               gmlp_w_out)
            pending = None
        else:
            h, a = _mla_layer(a, h, norm_g[i, 1], norm_g[i, 2], j, cos_t, sin_t, mla_w_dqkv,
                              mla_q_norm_g, mla_kv_norm_g, w_q_nope, w_q_rope, w_kv, mla_w_o,
                              batch, seq)
        f = _ffn(a, rows, i, ffn_w_gate_up, ffn_w_down)
        if i + 1 < DEPTH and (i + 1) % 2 == 0:
            pending, a = (f, norm_g[i, 3]), None
        else:
            g_next = norm_g[i + 1, 0] if i + 1 < DEPTH else None
            h, a = _postnorm(f, h, norm_g[i, 3], g_next)
    return h.reshape(batch, seq, d)
```
